```python
import jax, jax.numpy as jnp
from jax import lax
import numpy as np

D_MODEL = 1024
BATCH = 4
SEQ = 8192
DEPTH = 4

HEAD_DIM = 64
N_HEADS_A = 8
N_HEADS_B = 8
N_HEADS = N_HEADS_A + N_HEADS_B
ROPE_THETA = 10000.0
DILATED_PATTERNS = ((128, 1), (512, 4), (2048, 16))
QBLK = 64
GRID_W = 64
NA_ROWS = 8
NA_COLS = 16
POOL_WINDOWS = (2, 4, 8, 16)
N_POOL = len(POOL_WINDOWS)
POOL_CH = D_MODEL // N_POOL
N_EXPERTS = 16
N_EXPERT_GROUPS = 4
EXPERTS_PER_GROUP = N_EXPERTS // N_EXPERT_GROUPS
TOP_K = 2
D_EXPERT = D_MODEL
EBLK = 256
N_EVEN = (DEPTH + 1) // 2
N_ODD = DEPTH // 2
DEEPNORM_ALPHA = (2.0 * DEPTH) ** 0.25
DEEPNORM_BETA = (8.0 * DEPTH) ** -0.25
LN_EPS = 1e-5
NEG_INF = -1e30

kernel_name = 'hybrid_dilated_neighbourhood_pool_moe_encoder'


def _layer_norm(x, g, b):
    xf = x.astype(jnp.float32)
    mu = jnp.mean(xf, axis=-1, keepdims=True)
    var = jnp.mean(jnp.square(xf - mu), axis=-1, keepdims=True)
    y = (xf - mu) * lax.rsqrt(var + LN_EPS) * g.astype(jnp.float32) + b.astype(jnp.float32)
    return y.astype(x.dtype)


def _rope_tables(seq):
    pos = jnp.arange(seq, dtype=jnp.float32)
    inv_freq = ROPE_THETA ** (-jnp.arange(0, HEAD_DIM, 2, dtype=jnp.float32) / HEAD_DIM)
    ang = pos[:, None] * inv_freq[None, :]
    return jnp.cos(ang), jnp.sin(ang)


def _rope(t, cos, sin):
    half = HEAD_DIM // 2
    c = cos[None, :, None, :].astype(t.dtype)
    s = sin[None, :, None, :].astype(t.dtype)
    t1, t2 = t[..., :half], t[..., half:]
    return jnp.concatenate([t1 * c - t2 * s, t2 * c + t1 * s], axis=-1)


def _band_attention(q, k, v, radius):
    n, length, h, hd = q.shape
    nb = -(-length // QBLK)
    lp = nb * QBLK
    kw = QBLK + 2 * radius
    qp = jnp.pad(q, ((0, 0), (0, lp - length), (0, 0), (0, 0)))
    kv_pad = ((0, 0), (radius, radius + lp - length), (0, 0), (0, 0))
    kp = jnp.pad(k, kv_pad)
    vp = jnp.pad(v, kv_pad)
    kidx = np.arange(nb)[:, None] * QBLK + np.arange(kw)[None, :]
    kpos = kidx - radius
    qpos = np.arange(lp).reshape(nb, QBLK)
    mask = ((np.abs(kpos[:, None, :] - qpos[:, :, None]) <= radius)
            & (kpos[:, None, :] >= 0) & (kpos[:, None, :] < length))
    kb = kp[:, kidx]
    vb = vp[:, kidx]
    s = jnp.einsum('nbqhd,nbkhd->nhbqk', qp.reshape(n, nb, QBLK, h, hd), kb).astype(jnp.float32)
    s = jnp.where(mask, s, NEG_INF)
    m = jnp.max(s, axis=-1, keepdims=True)
    p = jnp.exp(s - m)
    den = jnp.sum(p, axis=-1)
    o = jnp.einsum('nhbqk,nbkhd->nbqhd', p.astype(v.dtype), vb).astype(jnp.float32)
    o = o / jnp.moveaxis(den, 1, 3)[..., None]
    lse = jnp.moveaxis(m[..., 0] + jnp.log(den), 1, 3)
    o = o.reshape(n, lp, h, hd)[:, :length]
    lse = lse.reshape(n, lp, h)[:, :length]
    return o, lse


def _dilated_attention(q, k, v, window, dilation):
    b, s, h, hd = q.shape
    sub = s // dilation
    radius = window // (2 * dilation)

    def split(t):
        return t.reshape(b, sub, dilation, h, hd).transpose(0, 2, 1, 3, 4).reshape(b * dilation, sub, h, hd)

    o, lse = _band_attention(split(q), split(k), split(v), radius)
    o = o.reshape(b, dilation, sub, h, hd).transpose(0, 2, 1, 3, 4).reshape(b, s, h, hd)
    lse = lse.reshape(b, dilation, sub, h).transpose(0, 2, 1, 3).reshape(b, s, h)
    return o, lse


def _neighbourhood_attention(q, k, v, rpb):
    b, s, h, hd = q.shape
    rows = s // GRID_W
    wr = min(NA_ROWS, rows)
    r = np.arange(rows)
    ridx = np.clip(r - wr // 2, 0, rows - wr)[:, None] + np.arange(wr)[None, :]
    c = np.arange(GRID_W)
    cstart = np.clip(c - NA_COLS // 2, 0, GRID_W - NA_COLS)
    colmask = (c[None, :] >= cstart[:, None]) & (c[None, :] < cstart[:, None] + NA_COLS)
    mask = np.broadcast_to(colmask[:, None, :], (GRID_W, wr, GRID_W)).reshape(GRID_W, wr * GRID_W)
    roff = ridx - r[:, None] + (NA_ROWS - 1)
    coff = np.clip(c[None, :] - c[:, None], 1 - NA_COLS, NA_COLS - 1) + (NA_COLS - 1)
    bias = rpb[:, roff[:, None, :, None], coff[None, :, None, :]]
    bias = bias.reshape(h, rows, GRID_W, wr * GRID_W).astype(jnp.float32)
    qg = q.reshape(b, rows, GRID_W, h, hd)
    kg = k.reshape(b, rows, GRID_W, h, hd)[:, ridx].reshape(b, rows, wr * GRID_W, h, hd)
    vg = v.reshape(b, rows, GRID_W, h, hd)[:, ridx].reshape(b, rows, wr * GRID_W, h, hd)
    sc = jnp.einsum('brqhd,brkhd->bhrqk', qg, kg).astype(jnp.float32) + bias[None]
    sc = jnp.where(mask, sc, NEG_INF)
    p = jax.nn.softmax(sc, axis=-1)
    o = jnp.einsum('bhrqk,brkhd->brqhd', p.astype(v.dtype), vg)
    return o.reshape(b, s, h, hd)


def _attention_mixer(x, w_in, w_out, rpb, cos, sin):
    b, s, d = x.shape
    qkv = (x @ w_in).reshape(b, s, 3, N_HEADS, HEAD_DIM)
    q = qkv[:, :, 0] * (HEAD_DIM ** -0.5)
    k = qkv[:, :, 1]
    v = qkv[:, :, 2]
    qa = _rope(q[:, :, :N_HEADS_A], cos, sin)
    ka = _rope(k[:, :, :N_HEADS_A], cos, sin)
    va = v[:, :, :N_HEADS_A]
    outs, lses = [], []
    for window, dilation in DILATED_PATTERNS:
        o_i, l_i = _dilated_attention(qa, ka, va, window, dilation)
        outs.append(o_i)
        lses.append(l_i)
    wts = jax.nn.softmax(jnp.stack(lses), axis=0)
    oa = jnp.sum(wts[..., None] * jnp.stack(outs), axis=0).astype(x.dtype)
    ob = _neighbourhood_attention(q[:, :, N_HEADS_A:], k[:, :, N_HEADS_A:], v[:, :, N_HEADS_A:], rpb)
    o = jnp.concatenate([oa, ob], axis=2).reshape(b, s, d)
    return o @ w_out


def _pool_mixer(x, w, scale):
    b, s, d = x.shape
    xg = x.reshape(b, s, N_POOL, POOL_CH).astype(jnp.float32)
    cs = jnp.pad(jnp.cumsum(xg, axis=1), ((0, 0), (1, 0), (0, 0), (0, 0)))
    t = np.arange(s)[:, None]
    half = np.array(POOL_WINDOWS)[None, :] // 2
    lo = np.clip(t - half, 0, s)
    hi = np.clip(t + half, 0, s)
    gi = np.arange(N_POOL)[None, :]
    cnt = (hi - lo).astype(np.float32)[None, :, :, None]
    mean = (cs[:, hi, gi] - cs[:, lo, gi]) / cnt
    u = (mean - xg).astype(x.dtype)
    y = jnp.einsum('bsgc,gce->bsge', u, w).reshape(b, s, d)
    return y * scale


def _moe(xt, router_w, router_bias, w1, w3, w2):
    t_count, d = xt.shape
    aff = jax.nn.sigmoid(xt.astype(jnp.float32) @ router_w.astype(jnp.float32))
    sel = (aff + router_bias.astype(jnp.float32)).reshape(t_count, N_EXPERT_GROUPS, EXPERTS_PER_GROUP)
    gscore = jnp.sum(lax.top_k(sel, 2)[0], axis=-1)
    grp = jnp.argmax(gscore, axis=-1)
    sel_g = sel[jnp.arange(t_count), grp]
    _, local = lax.top_k(sel_g, TOP_K)
    eidx = grp[:, None] * EXPERTS_PER_GROUP + local
    gate = jnp.take_along_axis(aff, eidx, axis=1)
    gate = gate / jnp.sum(gate, axis=-1, keepdims=True)
    m = t_count * TOP_K
    flat_e = eidx.reshape(m)
    flat_t = jnp.arange(m, dtype=jnp.int32) // TOP_K
    order = jnp.argsort(flat_e)
    se = flat_e[order]
    st = flat_t[order]
    sg = gate.reshape(m)[order]
    counts = jnp.zeros((N_EXPERTS,), jnp.int32).at[flat_e].add(1)
    start = jnp.cumsum(counts) - counts
    padded = (counts + EBLK - 1) // EBLK * EBLK
    pend = jnp.cumsum(padded)
    pstart = pend - padded
    dest = pstart[se] + jnp.arange(m, dtype=jnp.int32) - start[se]
    p_rows = m + N_EXPERTS * EBLK
    nblk = p_rows // EBLK
    buf = jnp.zeros((p_rows, d), xt.dtype).at[dest].set(xt[st])
    blk_e = jnp.minimum(jnp.searchsorted(pend, jnp.arange(nblk, dtype=jnp.int32) * EBLK, side='right'), N_EXPERTS - 1)

    def expert_block(args):
        xb, e = args
        hb = jax.nn.silu(xb @ w1[e]) * (xb @ w3[e])
        return hb @ w2[e]

    yb = lax.map(expert_block, (buf.reshape(nblk, EBLK, d), blk_e)).reshape(p_rows, d)
    y = jnp.zeros((t_count, d), jnp.float32).at[st].add(yb[dest].astype(jnp.float32) * sg[:, None])
    return y.astype(xt.dtype)


def setup_inputs(seed: int = 0) -> dict:
    key = jax.random.key(seed)
    ks = jax.random.split(key, 13)
    d = D_MODEL
    x = jax.random.normal(ks[0], (BATCH, SEQ, d), jnp.float32)
    w_in = jax.random.normal(ks[1], (N_EVEN, d, 3 * d), jnp.float32) * d ** -0.5
    w_out = jax.random.normal(ks[2], (N_EVEN, d, d), jnp.float32) * (d ** -0.5 * DEEPNORM_BETA)
    rpb = jax.random.normal(ks[3], (N_EVEN, N_HEADS_B, 2 * NA_ROWS - 1, 2 * NA_COLS - 1), jnp.float32) * 0.1
    pool_w = jax.random.normal(ks[4], (N_ODD, N_POOL, POOL_CH, POOL_CH), jnp.float32) * (POOL_CH ** -0.5 * DEEPNORM_BETA)
    pool_scale = 1.0 + 0.1 * jax.random.normal(ks[5], (N_ODD, d), jnp.float32)
    router_w = jax.random.normal(ks[6], (d, N_EXPERTS), jnp.float32) * d ** -0.5
    router_bias = 0.01 * jax.random.normal(ks[7], (N_EXPERTS,), jnp.float32)
    moe_w1 = jax.random.normal(ks[8], (DEPTH, N_EXPERTS, d, D_EXPERT), jnp.float32) * d ** -0.5
    moe_w3 = jax.random.normal(ks[9], (DEPTH, N_EXPERTS, d, D_EXPERT), jnp.float32) * d ** -0.5
    moe_w2 = jax.random.normal(ks[10], (DEPTH, N_EXPERTS, D_EXPERT, d), jnp.float32) * (D_EXPERT ** -0.5 * DEEPNORM_BETA)
    ln_g = 1.0 + 0.05 * jax.random.normal(ks[11], (DEPTH, 2, d), jnp.float32)
    ln_b = 0.02 * jax.random.normal(ks[12], (DEPTH, 2, d), jnp.float32)
    return {'x': x, 'w_in': w_in, 'w_out': w_out, 'rpb': rpb, 'pool_w': pool_w,
            'pool_scale': pool_scale, 'router_w': router_w, 'router_bias': router_bias,
            'moe_w1': moe_w1, 'moe_w3': moe_w3, 'moe_w2': moe_w2, 'ln_g': ln_g, 'ln_b': ln_b}


def reference(x, w_in, w_out, rpb, pool_w, pool_scale, router_w, router_bias, moe_w1, moe_w3, moe_w2, ln_g, ln_b):
    b, s, d = x.shape
    cos, sin = _rope_tables(s)
    for layer in range(DEPTH):
        i = layer // 2
        if layer % 2 == 0:
            h = _attention_mixer(x, w_in[i], w_out[i], rpb[i], cos, sin)
        else:
            h = _pool_mixer(x, pool_w[i], pool_scale[i])
        x = _layer_norm(DEEPNORM_ALPHA * x + h, ln_g[layer, 0], ln_b[layer, 0])
        f = _moe(x.reshape(b * s, d), router_w, router_bias, moe_w1[layer], moe_w3[layer], moe_w2[layer]).reshape(b, s, d)
        x = _layer_norm(DEEPNORM_ALPHA * x + f, ln_g[layer, 1], ln_b[layer, 1])
    return x
```

```python
import functools

import numpy as np
import jax
import jax.numpy as jnp
from jax import lax
from jax.experimental import pallas as pl
from jax.experimental.pallas import tpu as pltpu

D_MODEL = 1024
HEAD_DIM = 64
N_HEADS_A = 8
N_HEADS_B = 8
HALF_D = N_HEADS_A * HEAD_DIM
ROPE_THETA = 10000.0
DILATED_PATTERNS = ((128, 1), (512, 4), (2048, 16))
BAND_RADIUS = 64
GRID_W = 64
NA_ROWS = 8
NA_COLS = 16
POOL_WINDOWS = (2, 4, 8, 16)
POOL_CH = D_MODEL // len(POOL_WINDOWS)
POOL_HALO = 8
N_EXPERTS = 16
EXPERTS_PER_GROUP = 4
DEPTH = 4
DEEPNORM_ALPHA = (2.0 * DEPTH) ** 0.25
LN_EPS = 1e-5
NEG_INF = -1e30

LANES = 128
GATE_COLS = LANES
ROW_W = D_MODEL + GATE_COLS

_PAIR_CHAIN = ((0, 1), (0, 2), (1, 2), (1, 3), (0, 3), (2, 3))
CLASS_EA = tuple(g * EXPERTS_PER_GROUP + a for g in range(4) for a, _ in _PAIR_CHAIN)
CLASS_EB = tuple(g * EXPERTS_PER_GROUP + b for g in range(4) for _, b in _PAIR_CHAIN)
N_CLASSES = len(CLASS_EA)
CLASS_ROWS = 32

TM_QKV = 512
TN_QKV = 512
TQ_A = 128
TB_ROWS = 8
TM_LN = 256
TM_POOL = 256
TR_ROUTE = 512
TM_FFN = 256
CH_ROWS = 256
VMEM_LIMIT = 48 * 1024 * 1024

_NT = (((1,), (1,)), ((), ()))


def _cparams(sem, vmem=None):
    return pltpu.CompilerParams(dimension_semantics=sem, vmem_limit_bytes=vmem)


def _layer_norm(z, g, b):
    mu = jnp.mean(z, axis=-1, keepdims=True)
    zc = z - mu
    var = jnp.mean(zc * zc, axis=-1, keepdims=True)
    return zc * lax.rsqrt(var + LN_EPS) * g + b


def _qkv_kernel(x_ref, w_ref, cos_ref, sin_ref, o_ref):
    n = pl.program_id(1)
    acc = jnp.dot(x_ref[...].astype(jnp.bfloat16), w_ref[...], preferred_element_type=jnp.float32)
    scale = jnp.where(n < 2, HEAD_DIM ** -0.5, 1.0).astype(jnp.float32)

    @pl.when((n == 0) | (n == 2))
    def _():
        cos = cos_ref[...]
        sin = sin_ref[...]
        lane = lax.broadcasted_iota(jnp.int32, cos.shape, 1)
        first = (lane % HEAD_DIM) < (HEAD_DIM // 2)
        for c in range(TN_QKV // LANES):
            a = acc[:, c * LANES:(c + 1) * LANES]
            nxt = pltpu.roll(a, LANES - HEAD_DIM // 2, 1)
            prv = pltpu.roll(a, HEAD_DIM // 2, 1)
            rot = a * cos + jnp.where(first, -nxt, prv) * sin
            o_ref[:, c * LANES:(c + 1) * LANES] = (rot * scale).astype(o_ref.dtype)

    @pl.when((n != 0) & (n != 2))
    def _():
        o_ref[...] = (acc * scale).astype(o_ref.dtype)


def _qkv_proj(x2, w_bf, cos_t, sin_t, seq):
    t = x2.shape[0]
    tiles_per_seq = seq // TM_QKV
    return pl.pallas_call(
        _qkv_kernel,
        grid=(t // TM_QKV, 3 * D_MODEL // TN_QKV),
        in_specs=[
            pl.BlockSpec((TM_QKV, D_MODEL), lambda i, n: (i, 0)),
            pl.BlockSpec((D_MODEL, TN_QKV), lambda i, n: (0, n)),
            pl.BlockSpec((TM_QKV, LANES), lambda i, n: (i % tiles_per_seq, 0)),
            pl.BlockSpec((TM_QKV, LANES), lambda i, n: (i % tiles_per_seq, 0)),
        ],
        out_specs=pl.BlockSpec((TM_QKV, TN_QKV), lambda i, n: (i, n)),
        out_shape=jax.ShapeDtypeStruct((t, 3 * D_MODEL), jnp.bfloat16),
        compiler_params=_cparams(("parallel", "arbitrary"), VMEM_LIMIT),
        name="qkv_proj",
    )(x2, w_bf, cos_t, sin_t)


def _attn_a_kernel(*refs, first, nblk):
    if first:
        q_ref, kl_ref, km_ref, kr_ref, vl_ref, vm_ref, vr_ref, o_ref, l_ref = refs
        op_ref = lp_ref = None
    else:
        q_ref, kl_ref, km_ref, kr_ref, vl_ref, vm_ref, vr_ref, op_ref, lp_ref, o_ref, l_ref = refs
    j = pl.program_id(2)
    tq = q_ref.shape[0]
    tk = tq + 2 * BAND_RADIUS
    row = lax.broadcasted_iota(jnp.int32, (tq, tk), 0)
    col = lax.broadcasted_iota(jnp.int32, (tq, tk), 1)
    rel = col - row
    valid = (rel >= 0) & (rel <= 2 * BAND_RADIUS)
    valid = valid & ((col >= BAND_RADIUS) | (j > 0))
    valid = valid & ((col < tq + BAND_RADIUS) | (j < nblk - 1))
    lane = lax.broadcasted_iota(jnp.int32, (tq, LANES), 1)
    lo = lane < HEAD_DIM

    q = q_ref[...]
    kcat = jnp.concatenate([kl_ref[...], km_ref[...], kr_ref[...]], axis=0)
    vcat = jnp.concatenate([vl_ref[...], vm_ref[...], vr_ref[...]], axis=0)
    l_tile = jnp.zeros((tq, LANES), jnp.float32)
    for hp in range(N_HEADS_A // 2):
        sl = slice(hp * LANES, (hp + 1) * LANES)
        q2, k2, v2 = q[:, sl], kcat[:, sl], vcat[:, sl]
        if not first:
            o_prev = op_ref[:, sl].astype(jnp.float32)
        halves = []
        for hh in range(2):
            h = hp * 2 + hh
            qm = jnp.where(lo if hh == 0 else jnp.logical_not(lo), q2, jnp.zeros_like(q2))
            s = lax.dot_general(qm, k2, _NT, preferred_element_type=jnp.float32)
            s = jnp.where(valid, s, NEG_INF)
            m = jnp.max(s, axis=1, keepdims=True)
            p = jnp.exp(s - m)
            ssum = jnp.sum(p, axis=1, keepdims=True)
            o2 = jnp.dot(p.astype(jnp.bfloat16), v2, preferred_element_type=jnp.float32)
            lse = m + jnp.log(ssum)
            if first:
                o_h = o2 / ssum
                l_new = lse
            else:
                l_prev = lp_ref[:, h:h + 1]
                m2 = jnp.maximum(l_prev, lse)
                wp = jnp.exp(l_prev - m2)
                wn = jnp.exp(lse - m2)
                den = wp + wn
                l_new = m2 + jnp.log(den)
                o_h = (wp / den) * o_prev + (wn / (den * ssum)) * o2
            halves.append(o_h)
            l_tile = jnp.where(lane == h, l_new, l_tile)
        o_ref[:, sl] = jnp.where(lo, halves[0], halves[1]).astype(o_ref.dtype)
    l_ref[...] = l_tile


def _attn_a_branch(qkv, o_prev, l_prev, dil):
    b, s, _ = qkv.shape
    sub = s // dil
    tq = TQ_A
    nblk = sub // tq
    halo_blocks = sub // BAND_RADIUS
    per = tq // BAND_RADIUS
    first = o_prev is None
    ncol = 3 * D_MODEL // HALF_D
    qv = qkv.reshape(b, sub, dil * 3 * D_MODEL)

    def main_spec(cb):
        return pl.BlockSpec((None, tq, HALF_D), lambda bi, r, j: (bi, j, r * ncol + cb))

    def left_spec(cb):
        return pl.BlockSpec((None, BAND_RADIUS, HALF_D),
                            lambda bi, r, j: (bi, jnp.maximum(j * per - 1, 0), r * ncol + cb))

    def right_spec(cb):
        return pl.BlockSpec((None, BAND_RADIUS, HALF_D),
                            lambda bi, r, j: (bi, jnp.minimum((j + 1) * per, halo_blocks - 1), r * ncol + cb))

    o_spec = pl.BlockSpec((None, tq, HALF_D), lambda bi, r, j: (bi, j, r))
    l_spec = pl.BlockSpec((None, tq, LANES), lambda bi, r, j: (bi, j, r))
    in_specs = [main_spec(0), left_spec(2), main_spec(2), right_spec(2), left_spec(4), main_spec(4), right_spec(4)]
    args = [qv] * 7
    if not first:
        in_specs += [o_spec, l_spec]
        args += [o_prev.reshape(b, sub, dil * HALF_D), l_prev.reshape(b, sub, dil * LANES)]
    o, l = pl.pallas_call(
        functools.partial(_attn_a_kernel, first=first, nblk=nblk),
        grid=(b, dil, nblk),
        in_specs=in_specs,
        out_specs=[o_spec, l_spec],
        out_shape=[jax.ShapeDtypeStruct((b, sub, dil * HALF_D), jnp.bfloat16),
                   jax.ShapeDtypeStruct((b, sub, dil * LANES), jnp.float32)],
        compiler_params=_cparams(("parallel", "parallel", "parallel"), VMEM_LIMIT),
        name=f"attn_dilated_{dil}",
    )(*args)
    return o.reshape(b, s, HALF_D), l.reshape(b, s, LANES)


def _attn_b_kernel(q_ref, kp_ref, kc_ref, kn_ref, vp_ref, vc_ref, vn_ref, t2_ref, o_ref, kw_ref, vw_ref, *, rows):
    ib = pl.program_id(1)
    blk = TB_ROWS * GRID_W
    kw_ref[0:blk] = kp_ref[...]
    kw_ref[blk:2 * blk] = kc_ref[...]
    kw_ref[2 * blk:3 * blk] = kn_ref[...]
    vw_ref[0:blk] = vp_ref[...]
    vw_ref[blk:2 * blk] = vc_ref[...]
    vw_ref[2 * blk:3 * blk] = vn_ref[...]
    lane = lax.broadcasted_iota(jnp.int32, (GRID_W, LANES), 1)
    lo = lane < HEAD_DIM
    nkeys = NA_ROWS * GRID_W

    for hp in range(N_HEADS_B // 2):
        sl = slice(hp * LANES, (hp + 1) * LANES)

        def body(t, carry, sl=sl, hp=hp):
            i = ib * TB_ROWS + t
            rs = jnp.clip(i - NA_ROWS // 2, 0, rows - NA_ROWS)
            off = pl.multiple_of((rs - ib * TB_ROWS + TB_ROWS) * GRID_W, GRID_W)
            dlt = rs - i + (NA_ROWS - 1)
            qrow = pl.multiple_of(t * GRID_W, GRID_W)
            q2 = q_ref[pl.ds(qrow, GRID_W), sl]
            k2 = kw_ref[pl.ds(off, nkeys), sl]
            v2 = vw_ref[pl.ds(off, nkeys), sl]
            halves = []
            for hh in range(2):
                h = hp * 2 + hh
                qm = jnp.where(lo if hh == 0 else jnp.logical_not(lo), q2, jnp.zeros_like(q2))
                s = lax.dot_general(qm, k2, _NT, preferred_element_type=jnp.float32)
                bias = jnp.concatenate([t2_ref[h, dlt + 2 * kp] for kp in range(NA_ROWS // 2)], axis=1)
                s = s + bias
                m = jnp.max(s, axis=1, keepdims=True)
                p = jnp.exp(s - m)
                ssum = jnp.sum(p, axis=1, keepdims=True)
                o2 = jnp.dot(p.astype(jnp.bfloat16), v2, preferred_element_type=jnp.float32)
                halves.append(o2 / ssum)
            o_ref[pl.ds(qrow, GRID_W), sl] = jnp.where(lo, halves[0], halves[1]).astype(o_ref.dtype)
            return carry

        lax.fori_loop(0, TB_ROWS, body, 0)


def _bias_table(rpb):
    c = np.arange(GRID_W)
    cstart = np.clip(c - NA_COLS // 2, 0, GRID_W - NA_COLS)
    colmask = (c[None, :] >= cstart[:, None]) & (c[None, :] < cstart[:, None] + NA_COLS)
    coff = np.clip(c[None, :] - c[:, None], 1 - NA_COLS, NA_COLS - 1) + (NA_COLS - 1)
    bc = rpb[:, :, coff].astype(jnp.float32)
    bc = jnp.where(colmask[None, None], bc, NEG_INF)
    return jnp.concatenate([bc[:, :-1], bc[:, 1:]], axis=-1)


def _attn_b(qkv, t2):
    b, s, _ = qkv.shape
    rows = s // GRID_W
    blk = TB_ROWS * GRID_W
    nb = s // blk

    def spec(cb, shift):
        return pl.BlockSpec((None, blk, HALF_D),
                            lambda bi, ib: (bi, jnp.clip(ib + shift, 0, nb - 1), cb))

    return pl.pallas_call(
        functools.partial(_attn_b_kernel, rows=rows),
        grid=(b, nb),
        in_specs=[spec(1, 0), spec(3, -1), spec(3, 0), spec(3, 1), spec(5, -1), spec(5, 0), spec(5, 1),
                  pl.BlockSpec(t2.shape, lambda bi, ib: (0, 0, 0, 0))],
        out_specs=pl.BlockSpec((None, blk, HALF_D), lambda bi, ib: (bi, ib, 0)),
        out_shape=jax.ShapeDtypeStruct((b, s, HALF_D), jnp.bfloat16),
        scratch_shapes=[pltpu.VMEM((3 * blk, HALF_D), jnp.bfloat16), pltpu.VMEM((3 * blk, HALF_D), jnp.bfloat16)],
        compiler_params=_cparams(("parallel", "parallel"), VMEM_LIMIT),
        name="attn_neighbourhood",
    )(qkv, qkv, qkv, qkv, qkv, qkv, qkv, t2)


def _outln_kernel(oa_ref, ob_ref, x_ref, w_ref, g_ref, b_ref, o_ref):
    h = jnp.dot(oa_ref[...], w_ref[0:HALF_D, :], preferred_element_type=jnp.float32)
    h = h + jnp.dot(ob_ref[...], w_ref[HALF_D:D_MODEL, :], preferred_element_type=jnp.float32)
    z = DEEPNORM_ALPHA * x_ref[...] + h
    o_ref[:, 0:D_MODEL] = _layer_norm(z, g_ref[...], b_ref[...])
    o_ref[:, D_MODEL:ROW_W] = jnp.zeros((o_ref.shape[0], GATE_COLS), jnp.float32)


def _out_proj_ln(oa, ob, x2, w_bf, g, b):
    t = x2.shape[0]
    return pl.pallas_call(
        _outln_kernel,
        grid=(t // TM_LN,),
        in_specs=[
            pl.BlockSpec((TM_LN, HALF_D), lambda i: (i, 0)),
            pl.BlockSpec((TM_LN, HALF_D), lambda i: (i, 0)),
            pl.BlockSpec((TM_LN, D_MODEL), lambda i: (i, 0)),
            pl.BlockSpec((D_MODEL, D_MODEL), lambda i: (0, 0)),
            pl.BlockSpec((1, D_MODEL), lambda i: (0, 0)),
            pl.BlockSpec((1, D_MODEL), lambda i: (0, 0)),
        ],
        out_specs=pl.BlockSpec((TM_LN, ROW_W), lambda i: (i, 0)),
        out_shape=jax.ShapeDtypeStruct((t, ROW_W), jnp.float32),
        compiler_params=_cparams(("parallel",), VMEM_LIMIT),
        name="out_proj_ln",
    )(oa, ob, x2, w_bf, g, b)


def _poolln_kernel(xm_ref, xp_ref, xn_ref, w_ref, sc_ref, g_ref, b_ref, o_ref, xe_ref, *, seq):
    i = pl.program_id(0)
    tm = xm_ref.shape[0]
    tiles_per_seq = seq // tm
    it = i % tiles_per_seq
    x = xm_ref[...]
    xe_ref[0:POOL_HALO, :] = jnp.where(it > 0, xp_ref[...], 0.0)
    xe_ref[POOL_HALO:POOL_HALO + tm, :] = x
    xe_ref[POOL_HALO + tm:2 * POOL_HALO + tm, :] = jnp.where(it < tiles_per_seq - 1, xn_ref[...], 0.0)
    pos = it * tm + lax.broadcasted_iota(jnp.int32, (tm, 1), 0)
    ys = []
    for gi, win in enumerate(POOL_WINDOWS):
        half = win // 2
        cs = slice(gi * POOL_CH, (gi + 1) * POOL_CH)
        tot = xe_ref[POOL_HALO - half:POOL_HALO - half + tm, cs]
        for o in range(-half + 1, half):
            tot = tot + xe_ref[POOL_HALO + o:POOL_HALO + o + tm, cs]
        cnt = (jnp.minimum(pos + half, seq) - jnp.maximum(pos - half, 0)).astype(jnp.float32)
        u = tot / cnt - x[:, cs]
        ys.append(jnp.dot(u.astype(jnp.bfloat16), w_ref[gi], preferred_element_type=jnp.float32))
    y = jnp.concatenate(ys, axis=1) * sc_ref[...]
    z = DEEPNORM_ALPHA * x + y
    o_ref[:, 0:D_MODEL] = _layer_norm(z, g_ref[...], b_ref[...])
    o_ref[:, D_MODEL:ROW_W] = jnp.zeros((tm, GATE_COLS), jnp.float32)


def _pool_ln(x2, w_bf, scale, g, b, seq):
    t = x2.shape[0]
    hb = TM_POOL // POOL_HALO
    nhb = t // POOL_HALO
    return pl.pallas_call(
        functools.partial(_poolln_kernel, seq=seq),
        grid=(t // TM_POOL,),
        in_specs=[
            pl.BlockSpec((TM_POOL, D_MODEL), lambda i: (i, 0)),
            pl.BlockSpec((POOL_HALO, D_MODEL), lambda i: (jnp.maximum(i * hb - 1, 0), 0)),
            pl.BlockSpec((POOL_HALO, D_MODEL), lambda i: (jnp.minimum((i + 1) * hb, nhb - 1), 0)),
            pl.BlockSpec(w_bf.shape, lambda i: (0, 0, 0)),
            pl.BlockSpec((1, D_MODEL), lambda i: (0, 0)),
            pl.BlockSpec((1, D_MODEL), lambda i: (0, 0)),
            pl.BlockSpec((1, D_MODEL), lambda i: (0, 0)),
        ],
        out_specs=pl.BlockSpec((TM_POOL, ROW_W), lambda i: (i, 0)),
        out_shape=jax.ShapeDtypeStruct((t, ROW_W), jnp.float32),
        scratch_shapes=[pltpu.VMEM((TM_POOL + 2 * POOL_HALO, D_MODEL), jnp.float32)],
        compiler_params=_cparams(("parallel",), VMEM_LIMIT),
        name="pool_ln",
    )(x2, x2, x2, w_bf, scale, g, b)


def _route_kernel(x_ref, wr_ref, rb_ref, gate_ref, meta_ref, cnt_ref, base_ref):
    i = pl.program_id(0)
    tr = x_ref.shape[0]

    @pl.when(i == 0)
    def _():
        base_ref[...] = jnp.zeros_like(base_ref)

    x = x_ref[:, 0:D_MODEL]
    xhi = x.astype(jnp.bfloat16)
    xlo = (x - xhi.astype(jnp.float32)).astype(jnp.bfloat16)
    l1 = lax.dot_general(wr_ref[...], xhi, _NT, preferred_element_type=jnp.float32)
    l2 = lax.dot_general(wr_ref[0:N_EXPERTS, :], xlo, _NT, preferred_element_type=jnp.float32)
    logits = l1[0:N_EXPERTS] + l1[N_EXPERTS:2 * N_EXPERTS] + l2
    aff = 1.0 / (1.0 + jnp.exp(-logits))
    sel = aff + rb_ref[...]

    best = jnp.full((1, tr), -jnp.inf, jnp.float32)
    cls = jnp.zeros((1, tr), jnp.int32)
    aa = jnp.zeros((1, tr), jnp.float32)
    ab = jnp.zeros((1, tr), jnp.float32)
    for c in range(N_CLASSES):
        ea, eb = CLASS_EA[c], CLASS_EB[c]
        ps = sel[ea:ea + 1] + sel[eb:eb + 1]
        better = ps > best
        best = jnp.where(better, ps, best)
        cls = jnp.where(better, c, cls)
        aa = jnp.where(better, aff[ea:ea + 1], aa)
        ab = jnp.where(better, aff[eb:eb + 1], ab)
    ga = aa / (aa + ab)
    gb = ab / (aa + ab)

    crow = lax.broadcasted_iota(jnp.int32, (CLASS_ROWS, tr), 0)
    onehot = (crow == cls).astype(jnp.float32)
    r_i = lax.broadcasted_iota(jnp.int32, (tr, tr), 0)
    c_i = lax.broadcasted_iota(jnp.int32, (tr, tr), 1)
    upper = (r_i < c_i).astype(jnp.bfloat16)
    excl = jnp.dot(onehot.astype(jnp.bfloat16), upper, preferred_element_type=jnp.float32)
    base = base_ref[...]
    rank = jnp.sum(onehot * (excl + base[:, 0:1]), axis=0, keepdims=True)
    base = base + jnp.sum(onehot, axis=1, keepdims=True)
    base_ref[...] = base
    cnt_ref[...] = base.astype(jnp.int32)

    mrow = lax.broadcasted_iota(jnp.int32, (8, tr), 0)
    meta_ref[...] = jnp.where(mrow == 0, cls, jnp.where(mrow == 1, rank.astype(jnp.int32), 0))
    grow = lax.broadcasted_iota(jnp.int32, (LANES, tr), 0)
    gmat = jnp.where(grow == 0, ga, jnp.where(grow == 1, gb, 0.0))
    gate_ref[...] = gmat.T


def _route(xw, wr_t, rbias):
    t = xw.shape[0]
    return pl.pallas_call(
        _route_kernel,
        grid=(t // TR_ROUTE,),
        in_specs=[
            pl.BlockSpec((TR_ROUTE, ROW_W), lambda i: (i, 0)),
            pl.BlockSpec((2 * N_EXPERTS, D_MODEL), lambda i: (0, 0)),
            pl.BlockSpec((N_EXPERTS, 1), lambda i: (0, 0)),
        ],
        out_specs=[
            pl.BlockSpec((TR_ROUTE, GATE_COLS), lambda i: (i, D_MODEL // GATE_COLS)),
            pl.BlockSpec((8, TR_ROUTE), lambda i: (0, i)),
            pl.BlockSpec((CLASS_ROWS, LANES), lambda i: (0, 0)),
        ],
        out_shape=[
            jax.ShapeDtypeStruct((t, ROW_W), jnp.float32),
            jax.ShapeDtypeStruct((8, t), jnp.int32),
            jax.ShapeDtypeStruct((CLASS_ROWS, LANES), jnp.int32),
        ],
        scratch_shapes=[pltpu.VMEM((CLASS_ROWS, LANES), jnp.float32)],
        input_output_aliases={0: 0},
        compiler_params=_cparams(("arbitrary",), VMEM_LIMIT),
        name="route",
    )(xw, wr_t, rbias)


def _row_copy(src_hbm, dst_ref, src_row, dst_row, sem):
    return pltpu.make_async_copy(src_hbm.at[pl.ds(src_row, 1)], dst_ref.at[pl.ds(dst_row, 1)], sem)


def _dispatch_kernel(dest_ref, x_hbm, init_hbm, o_hbm, sem):
    del init_hbm
    i = pl.program_id(0)

    def issue(r, carry):
        tok = i * CH_ROWS + r
        _row_copy(x_hbm, o_hbm, tok, dest_ref[tok], sem).start()
        return carry

    lax.fori_loop(0, CH_ROWS, issue, 0)

    def drain(r, carry):
        _row_copy(x_hbm, o_hbm, 0, 0, sem).wait()
        return carry

    lax.fori_loop(0, CH_ROWS, drain, 0)


def _dispatch(dest, xw, p_rows):
    t = xw.shape[0]
    init = jnp.zeros((p_rows, ROW_W), jnp.float32)
    return pl.pallas_call(
        _dispatch_kernel,
        grid_spec=pltpu.PrefetchScalarGridSpec(
            num_scalar_prefetch=1,
            grid=(t // CH_ROWS,),
            in_specs=[pl.BlockSpec(memory_space=pl.ANY), pl.BlockSpec(memory_space=pl.ANY)],
            out_specs=pl.BlockSpec(memory_space=pl.ANY),
            scratch_shapes=[pltpu.SemaphoreType.DMA(())],
        ),
        out_shape=jax.ShapeDtypeStruct((p_rows, ROW_W), jnp.float32),
        input_output_aliases={2: 0},
        compiler_params=_cparams(("arbitrary",)),
        name="dispatch",
    )(dest, xw, init)


def _ffn_kernel(ta_ref, tb_ref, nv_ref, xs_ref, w1a_ref, w3a_ref, w2a_ref, w1b_ref, w3b_ref, w2b_ref, o_ref, h_ref):
    del ta_ref, tb_ref
    i = pl.program_id(0)

    @pl.when(i < nv_ref[0])
    def _():
        x = xs_ref[:, 0:D_MODEL].astype(jnp.bfloat16)
        chunk = 256
        for e, (w1_ref, w3_ref) in enumerate(((w1a_ref, w3a_ref), (w1b_ref, w3b_ref))):
            g = xs_ref[:, D_MODEL + e:D_MODEL + e + 1]
            for c in range(D_MODEL // chunk):
                cs = slice(c * chunk, (c + 1) * chunk)
                a = jnp.dot(x, w1_ref[:, cs], preferred_element_type=jnp.float32)
                bb = jnp.dot(x, w3_ref[:, cs], preferred_element_type=jnp.float32)
                hcol = (a / (1.0 + jnp.exp(-a))) * bb * g
                h_ref[:, e * D_MODEL + c * chunk:e * D_MODEL + (c + 1) * chunk] = hcol.astype(jnp.bfloat16)
        y = jnp.dot(h_ref[:, 0:D_MODEL], w2a_ref[...], preferred_element_type=jnp.float32)
        y = y + jnp.dot(h_ref[:, D_MODEL:2 * D_MODEL], w2b_ref[...], preferred_element_type=jnp.float32)
        o_ref[...] = y

    @pl.when(i >= nv_ref[0])
    def _():
        o_ref[...] = jnp.zeros_like(o_ref)


def _ffn(ta, tb, nvalid, xs, w1, w3, w2):
    p_rows = xs.shape[0]
    nt = p_rows // TM_FFN

    def wspec(which):
        if which == 0:
            return pl.BlockSpec((None, D_MODEL, D_MODEL), lambda i, ta, tb, nv: (ta[i], 0, 0))
        return pl.BlockSpec((None, D_MODEL, D_MODEL), lambda i, ta, tb, nv: (tb[i], 0, 0))

    return pl.pallas_call(
        _ffn_kernel,
        grid_spec=pltpu.PrefetchScalarGridSpec(
            num_scalar_prefetch=3,
            grid=(nt,),
            in_specs=[pl.BlockSpec((TM_FFN, ROW_W), lambda i, ta, tb, nv: (i, 0)),
                      wspec(0), wspec(0), wspec(0), wspec(1), wspec(1), wspec(1)],
            out_specs=pl.BlockSpec((TM_FFN, D_MODEL), lambda i, ta, tb, nv: (i, 0)),
            scratch_shapes=[pltpu.VMEM((TM_FFN, 2 * D_MODEL), jnp.bfloat16)],
        ),
        out_shape=jax.ShapeDtypeStruct((p_rows, D_MODEL), jnp.float32),
        compiler_params=_cparams(("arbitrary",), VMEM_LIMIT),
        name="expert_ffn",
    )(ta, tb, nvalid, xs, w1, w3, w2, w1, w3, w2)


def _combine_kernel(dest_ref, ys_hbm, x_ref, g_ref, b_ref, o_ref, ybuf_ref, sem):
    i = pl.program_id(0)

    def issue(r, carry):
        _row_copy(ys_hbm, ybuf_ref, dest_ref[i * CH_ROWS + r], r, sem).start()
        return carry

    lax.fori_loop(0, CH_ROWS, issue, 0)

    def drain(r, carry):
        _row_copy(ys_hbm, ybuf_ref, 0, 0, sem).wait()
        return carry

    lax.fori_loop(0, CH_ROWS, drain, 0)
    z = DEEPNORM_ALPHA * x_ref[:, 0:D_MODEL] + ybuf_ref[...]
    o_ref[...] = _layer_norm(z, g_ref[...], b_ref[...])


def _combine_ln(dest, ys, xw, g, b):
    t = xw.shape[0]
    return pl.pallas_call(
        _combine_kernel,
        grid_spec=pltpu.PrefetchScalarGridSpec(
            num_scalar_prefetch=1,
            grid=(t // CH_ROWS,),
            in_specs=[pl.BlockSpec(memory_space=pl.ANY),
                      pl.BlockSpec((CH_ROWS, ROW_W), lambda i, d: (i, 0)),
                      pl.BlockSpec((1, D_MODEL), lambda i, d: (0, 0)),
                      pl.BlockSpec((1, D_MODEL), lambda i, d: (0, 0))],
            out_specs=pl.BlockSpec((CH_ROWS, D_MODEL), lambda i, d: (i, 0)),
            scratch_shapes=[pltpu.VMEM((CH_ROWS, D_MODEL), jnp.float32), pltpu.SemaphoreType.DMA(())],
        ),
        out_shape=jax.ShapeDtypeStruct((t, D_MODEL), jnp.float32),
        compiler_params=_cparams(("arbitrary",), VMEM_LIMIT),
        name="combine_ln",
    )(dest, ys, xw, g, b)


def _moe_ln(xw, wr_t, rbias, w1, w3, w2, g, b):
    t = xw.shape[0]
    xw, meta, cnt = _route(xw, wr_t, rbias)
    cls, rank = meta[0], meta[1]
    counts = cnt[:N_CLASSES, 0]
    tiles_c = (counts + TM_FFN - 1) // TM_FFN
    tile_end = jnp.cumsum(tiles_c)
    row_start = (tile_end - tiles_c) * TM_FFN
    dest = row_start[cls] + rank
    nt = t // TM_FFN + N_CLASSES
    nvalid = tile_end[-1]
    tidx = jnp.arange(nt, dtype=jnp.int32)
    tcls = jnp.searchsorted(tile_end, jnp.minimum(tidx, nvalid - 1), side="right").astype(jnp.int32)
    ta = jnp.asarray(CLASS_EA, jnp.int32)[tcls]
    tb = jnp.asarray(CLASS_EB, jnp.int32)[tcls]
    xs = _dispatch(dest, xw, nt * TM_FFN)
    ys = _ffn(ta, tb, nvalid.reshape(1).astype(jnp.int32), xs, w1, w3, w2)
    return _combine_ln(dest, ys, xw, g, b)


def _rope_tables(seq):
    pos = jnp.arange(seq, dtype=jnp.float32)
    inv_freq = ROPE_THETA ** (-jnp.arange(0, HEAD_DIM, 2, dtype=jnp.float32) / HEAD_DIM)
    ang = pos[:, None] * inv_freq[None, :]
    reps = LANES // (HEAD_DIM // 2)
    return jnp.tile(jnp.cos(ang), (1, reps)), jnp.tile(jnp.sin(ang), (1, reps))


def kernel(x, w_in, w_out, rpb, pool_w, pool_scale, router_w, router_bias, moe_w1, moe_w3, moe_w2, ln_g, ln_b):
    b, s, d = x.shape
    t = b * s
    bf = jnp.bfloat16
    cos_t, sin_t = _rope_tables(s)
    rw_hi = router_w.astype(bf)
    rw_lo = (router_w - rw_hi.astype(jnp.float32)).astype(bf)
    wr_t = jnp.concatenate([rw_hi.T, rw_lo.T], axis=0)
    rbias = router_bias.astype(jnp.float32).reshape(N_EXPERTS, 1)
    w_in_bf, w_out_bf, pool_w_bf = w_in.astype(bf), w_out.astype(bf), pool_w.astype(bf)
    w1_bf, w3_bf, w2_bf = moe_w1.astype(bf), moe_w3.astype(bf), moe_w2.astype(bf)

    x2 = x.reshape(t, d)
    for layer in range(DEPTH):
        i = layer // 2
        g0, b0 = ln_g[layer, 0].reshape(1, d), ln_b[layer, 0].reshape(1, d)
        g1, b1 = ln_g[layer, 1].reshape(1, d), ln_b[layer, 1].reshape(1, d)
        if layer % 2 == 0:
            qkv = _qkv_proj(x2, w_in_bf[i], cos_t, sin_t, s).reshape(b, s, 3 * d)
            oa, la = None, None
            for _, dil in DILATED_PATTERNS:
                oa, la = _attn_a_branch(qkv, oa, la, dil)
            ob = _attn_b(qkv, _bias_table(rpb[i]))
            xw = _out_proj_ln(oa.reshape(t, HALF_D), ob.reshape(t, HALF_D), x2, w_out_bf[i], g0, b0)
        else:
            xw = _pool_ln(x2, pool_w_bf[i], pool_scale[i].reshape(1, d), g0, b0, s)
        x2 = _moe_ln(xw, wr_t, rbias, w1_bf[layer], w3_bf[layer], w2_bf[layer], g1, b1)
    return x2.reshape(b, s, d)
```

```python
import functools

import numpy as np
import jax
import jax.numpy as jnp
from jax import lax
from jax.experimental import pallas as pl
from jax.experimental.pallas import tpu as pltpu

D_MODEL = 1024
HEAD_DIM = 64
N_HEADS_A = 8
N_HEADS_B = 8
HALF_D = N_HEADS_A * HEAD_DIM
ROPE_THETA = 10000.0
DILATED_PATTERNS = ((128, 1), (512, 4), (2048, 16))
BAND_RADIUS = 64
GRID_W = 64
NA_ROWS = 8
NA_COLS = 16
POOL_WINDOWS = (2, 4, 8, 16)
POOL_CH = D_MODEL // len(POOL_WINDOWS)
POOL_HALO = 8
N_EXPERTS = 16
EXPERTS_PER_GROUP = 4
DEPTH = 4
DEEPNORM_ALPHA = (2.0 * DEPTH) ** 0.25
LN_EPS = 1e-5
NEG_INF = -1e30

LANES = 128
GATE_COLS = LANES
ROW_W = D_MODEL + GATE_COLS

_PAIR_CHAIN = ((0, 1), (0, 2), (1, 2), (1, 3), (0, 3), (2, 3))
CLASS_EA = tuple(g * EXPERTS_PER_GROUP + a for g in range(4) for a, _ in _PAIR_CHAIN)
CLASS_EB = tuple(g * EXPERTS_PER_GROUP + b for g in range(4) for _, b in _PAIR_CHAIN)
N_CLASSES = len(CLASS_EA)
CLASS_ROWS = 32

TM_QKV = 1024
TN_QKV = 512
TQ_A = 128
TB_ROWS = 8
TM_LN = 256
TM_POOL = 256
TR_ROUTE = 512
TM_FFN = 256
CH_ROWS = 256
VMEM_LIMIT = 48 * 1024 * 1024

_NT = (((1,), (1,)), ((), ()))


def _cparams(sem, vmem=None):
    return pltpu.CompilerParams(dimension_semantics=sem, vmem_limit_bytes=vmem)


def _layer_norm(z, g, b):
    mu = jnp.mean(z, axis=-1, keepdims=True)
    zc = z - mu
    var = jnp.mean(zc * zc, axis=-1, keepdims=True)
    return zc * lax.rsqrt(var + LN_EPS) * g + b


def _qkv_kernel(x_ref, w_ref, cos_ref, sin_ref, o_ref, xb_ref):
    n = pl.program_id(1)

    @pl.when(n == 0)
    def _():
        xb_ref[...] = x_ref[...].astype(jnp.bfloat16)

    acc = jnp.dot(xb_ref[...], w_ref[...], preferred_element_type=jnp.float32)
    scale = jnp.where(n < 2, HEAD_DIM ** -0.5, 1.0).astype(jnp.float32)

    @pl.when((n == 0) | (n == 2))
    def _():
        cos = cos_ref[...]
        sin = sin_ref[...]
        lane = lax.broadcasted_iota(jnp.int32, cos.shape, 1)
        first = (lane % HEAD_DIM) < (HEAD_DIM // 2)
        for c in range(TN_QKV // LANES):
            a = acc[:, c * LANES:(c + 1) * LANES]
            nxt = pltpu.roll(a, LANES - HEAD_DIM // 2, 1)
            prv = pltpu.roll(a, HEAD_DIM // 2, 1)
            rot = a * cos + jnp.where(first, -nxt, prv) * sin
            o_ref[:, c * LANES:(c + 1) * LANES] = (rot * scale).astype(o_ref.dtype)

    @pl.when((n != 0) & (n != 2))
    def _():
        o_ref[...] = (acc * scale).astype(o_ref.dtype)


def _qkv_proj(x2, w_bf, cos_t, sin_t, seq):
    t = x2.shape[0]
    tiles_per_seq = seq // TM_QKV
    return pl.pallas_call(
        _qkv_kernel,
        grid=(t // TM_QKV, 3 * D_MODEL // TN_QKV),
        in_specs=[
            pl.BlockSpec((TM_QKV, D_MODEL), lambda i, n: (i, 0)),
            pl.BlockSpec((D_MODEL, TN_QKV), lambda i, n: (0, n)),
            pl.BlockSpec((TM_QKV, LANES), lambda i, n: (i % tiles_per_seq, 0)),
            pl.BlockSpec((TM_QKV, LANES), lambda i, n: (i % tiles_per_seq, 0)),
        ],
        out_specs=pl.BlockSpec((TM_QKV, TN_QKV), lambda i, n: (i, n)),
        out_shape=jax.ShapeDtypeStruct((t, 3 * D_MODEL), jnp.bfloat16),
        scratch_shapes=[pltpu.VMEM((TM_QKV, D_MODEL), jnp.bfloat16)],
        compiler_params=_cparams(("parallel", "arbitrary"), VMEM_LIMIT),
        name="qkv_proj",
    )(x2, w_bf, cos_t, sin_t)


def _attn_a_kernel(*refs, first, nblk):
    if first:
        q_ref, kl_ref, km_ref, kr_ref, vl_ref, vm_ref, vr_ref, o_ref, l_ref = refs
        op_ref = lp_ref = None
    else:
        q_ref, kl_ref, km_ref, kr_ref, vl_ref, vm_ref, vr_ref, op_ref, lp_ref, o_ref, l_ref = refs
    j = pl.program_id(2)
    tq = q_ref.shape[0]
    tk = tq + 2 * BAND_RADIUS
    row = lax.broadcasted_iota(jnp.int32, (tq, tk), 0)
    col = lax.broadcasted_iota(jnp.int32, (tq, tk), 1)
    rel = col - row
    valid = (rel >= 0) & (rel <= 2 * BAND_RADIUS)
    valid = valid & ((col >= BAND_RADIUS) | (j > 0))
    valid = valid & ((col < tq + BAND_RADIUS) | (j < nblk - 1))
    lane = lax.broadcasted_iota(jnp.int32, (tq, LANES), 1)
    lo = lane < HEAD_DIM

    q = q_ref[...]
    kcat = jnp.concatenate([kl_ref[...], km_ref[...], kr_ref[...]], axis=0)
    vcat = jnp.concatenate([vl_ref[...], vm_ref[...], vr_ref[...]], axis=0)
    m_tile = jnp.zeros((tq, LANES), jnp.float32)
    s_tile = jnp.ones((tq, LANES), jnp.float32)
    pair_out = []
    for hp in range(N_HEADS_A // 2):
        sl = slice(hp * LANES, (hp + 1) * LANES)
        q2, k2, v2 = q[:, sl], kcat[:, sl], vcat[:, sl]
        halves = []
        for hh in range(2):
            h = hp * 2 + hh
            qm = jnp.where(lo if hh == 0 else jnp.logical_not(lo), q2, jnp.zeros_like(q2))
            s = lax.dot_general(qm, k2, _NT, preferred_element_type=jnp.float32)
            s = jnp.where(valid, s, NEG_INF)
            m = jnp.max(s, axis=1, keepdims=True)
            p = jnp.exp(s - m)
            ssum = jnp.sum(p, axis=1, keepdims=True)
            halves.append(jnp.dot(p.astype(jnp.bfloat16), v2, preferred_element_type=jnp.float32))
            m_tile = jnp.where(lane == h, m, m_tile)
            s_tile = jnp.where(lane == h, ssum, s_tile)
        pair_out.append(jnp.where(lo, halves[0], halves[1]))

    lse = m_tile + jnp.log(s_tile)
    if first:
        c_new = 1.0 / s_tile
        l_new = lse
    else:
        l_prev = lp_ref[...]
        m2 = jnp.maximum(l_prev, lse)
        wp = jnp.exp(l_prev - m2)
        wn = jnp.exp(lse - m2)
        den = wp + wn
        l_new = m2 + jnp.log(den)
        c_prev = wp / den
        c_new = wn / (den * s_tile)
    for hp in range(N_HEADS_A // 2):
        sl = slice(hp * LANES, (hp + 1) * LANES)
        h0 = 2 * hp
        out = jnp.where(lo, c_new[:, h0:h0 + 1], c_new[:, h0 + 1:h0 + 2]) * pair_out[hp]
        if not first:
            cp2 = jnp.where(lo, c_prev[:, h0:h0 + 1], c_prev[:, h0 + 1:h0 + 2])
            out = out + cp2 * op_ref[:, sl].astype(jnp.float32)
        o_ref[:, sl] = out.astype(o_ref.dtype)
    l_ref[...] = l_new


def _attn_a_branch(qkv, o_prev, l_prev, dil):
    b, s, _ = qkv.shape
    sub = s // dil
    tq = TQ_A
    nblk = sub // tq
    halo_blocks = sub // BAND_RADIUS
    per = tq // BAND_RADIUS
    first = o_prev is None
    ncol = 3 * D_MODEL // HALF_D
    qv = qkv.reshape(b, sub, dil * 3 * D_MODEL)

    def main_spec(cb):
        return pl.BlockSpec((None, tq, HALF_D), lambda bi, r, j: (bi, j, r * ncol + cb))

    def left_spec(cb):
        return pl.BlockSpec((None, BAND_RADIUS, HALF_D),
                            lambda bi, r, j: (bi, jnp.maximum(j * per - 1, 0), r * ncol + cb))

    def right_spec(cb):
        return pl.BlockSpec((None, BAND_RADIUS, HALF_D),
                            lambda bi, r, j: (bi, jnp.minimum((j + 1) * per, halo_blocks - 1), r * ncol + cb))

    o_spec = pl.BlockSpec((None, tq, HALF_D), lambda bi, r, j: (bi, j, r))
    l_spec = pl.BlockSpec((None, tq, LANES), lambda bi, r, j: (bi, j, r))
    in_specs = [main_spec(0), left_spec(2), main_spec(2), right_spec(2), left_spec(4), main_spec(4), right_spec(4)]
    args = [qv] * 7
    if not first:
        in_specs += [o_spec, l_spec]
        args += [o_prev.reshape(b, sub, dil * HALF_D), l_prev.reshape(b, sub, dil * LANES)]
    o, l = pl.pallas_call(
        functools.partial(_attn_a_kernel, first=first, nblk=nblk),
        grid=(b, dil, nblk),
        in_specs=in_specs,
        out_specs=[o_spec, l_spec],
        out_shape=[jax.ShapeDtypeStruct((b, sub, dil * HALF_D), jnp.bfloat16),
                   jax.ShapeDtypeStruct((b, sub, dil * LANES), jnp.float32)],
        compiler_params=_cparams(("parallel", "parallel", "parallel"), VMEM_LIMIT),
        name=f"attn_dilated_{dil}",
    )(*args)
    return o.reshape(b, s, HALF_D), l.reshape(b, s, LANES)


def _attn_b_kernel(q_ref, kp_ref, kc_ref, kn_ref, vp_ref, vc_ref, vn_ref, t2_ref, o_ref, kw_ref, vw_ref, *, rows):
    ib = pl.program_id(1)
    blk = TB_ROWS * GRID_W
    kw_ref[0:blk] = kp_ref[...]
    kw_ref[blk:2 * blk] = kc_ref[...]
    kw_ref[2 * blk:3 * blk] = kn_ref[...]
    vw_ref[0:blk] = vp_ref[...]
    vw_ref[blk:2 * blk] = vc_ref[...]
    vw_ref[2 * blk:3 * blk] = vn_ref[...]
    lane = lax.broadcasted_iota(jnp.int32, (GRID_W, LANES), 1)
    lo = lane < HEAD_DIM
    nkeys = NA_ROWS * GRID_W

    for hp in range(N_HEADS_B // 2):
        sl = slice(hp * LANES, (hp + 1) * LANES)

        def body(t, carry, sl=sl, hp=hp):
            i = ib * TB_ROWS + t
            rs = jnp.clip(i - NA_ROWS // 2, 0, rows - NA_ROWS)
            off = pl.multiple_of((rs - ib * TB_ROWS + TB_ROWS) * GRID_W, GRID_W)
            dlt = rs - i + (NA_ROWS - 1)
            qrow = pl.multiple_of(t * GRID_W, GRID_W)
            q2 = q_ref[pl.ds(qrow, GRID_W), sl]
            k2 = kw_ref[pl.ds(off, nkeys), sl]
            v2 = vw_ref[pl.ds(off, nkeys), sl]
            halves = []
            for hh in range(2):
                h = hp * 2 + hh
                qm = jnp.where(lo if hh == 0 else jnp.logical_not(lo), q2, jnp.zeros_like(q2))
                s = lax.dot_general(qm, k2, _NT, preferred_element_type=jnp.float32)
                bias = jnp.concatenate([t2_ref[h, dlt + 2 * kp] for kp in range(NA_ROWS // 2)], axis=1)
                s = s + bias
                m = jnp.max(s, axis=1, keepdims=True)
                p = jnp.exp(s - m)
                ssum = jnp.sum(p, axis=1, keepdims=True)
                o2 = jnp.dot(p.astype(jnp.bfloat16), v2, preferred_element_type=jnp.float32)
                halves.append(o2 / ssum)
            o_ref[pl.ds(qrow, GRID_W), sl] = jnp.where(lo, halves[0], halves[1]).astype(o_ref.dtype)
            return carry

        lax.fori_loop(0, TB_ROWS, body, 0, unroll=True)


def _bias_table(rpb):
    c = np.arange(GRID_W)
    cstart = np.clip(c - NA_COLS // 2, 0, GRID_W - NA_COLS)
    colmask = (c[None, :] >= cstart[:, None]) & (c[None, :] < cstart[:, None] + NA_COLS)
    coff = np.clip(c[None, :] - c[:, None], 1 - NA_COLS, NA_COLS - 1) + (NA_COLS - 1)
    bc = rpb[:, :, coff].astype(jnp.float32)
    bc = jnp.where(colmask[None, None], bc, NEG_INF)
    return jnp.concatenate([bc[:, :-1], bc[:, 1:]], axis=-1)


def _attn_b(qkv, t2):
    b, s, _ = qkv.shape
    rows = s // GRID_W
    blk = TB_ROWS * GRID_W
    nb = s // blk

    def spec(cb, shift):
        return pl.BlockSpec((None, blk, HALF_D),
                            lambda bi, ib: (bi, jnp.clip(ib + shift, 0, nb - 1), cb))

    return pl.pallas_call(
        functools.partial(_attn_b_kernel, rows=rows),
        grid=(b, nb),
        in_specs=[spec(1, 0), spec(3, -1), spec(3, 0), spec(3, 1), spec(5, -1), spec(5, 0), spec(5, 1),
                  pl.BlockSpec(t2.shape, lambda bi, ib: (0, 0, 0, 0))],
        out_specs=pl.BlockSpec((None, blk, HALF_D), lambda bi, ib: (bi, ib, 0)),
        out_shape=jax.ShapeDtypeStruct((b, s, HALF_D), jnp.bfloat16),
        scratch_shapes=[pltpu.VMEM((3 * blk, HALF_D), jnp.bfloat16), pltpu.VMEM((3 * blk, HALF_D), jnp.bfloat16)],
        compiler_params=_cparams(("parallel", "parallel"), VMEM_LIMIT),
        name="attn_neighbourhood",
    )(qkv, qkv, qkv, qkv, qkv, qkv, qkv, t2)


def _outln_kernel(oa_ref, ob_ref, x_ref, w_ref, g_ref, b_ref, o_ref):
    h = jnp.dot(oa_ref[...], w_ref[0:HALF_D, :], preferred_element_type=jnp.float32)
    h = h + jnp.dot(ob_ref[...], w_ref[HALF_D:D_MODEL, :], preferred_element_type=jnp.float32)
    z = DEEPNORM_ALPHA * x_ref[...] + h
    o_ref[:, 0:D_MODEL] = _layer_norm(z, g_ref[...], b_ref[...])
    o_ref[:, D_MODEL:ROW_W] = jnp.zeros((o_ref.shape[0], GATE_COLS), jnp.float32)


def _out_proj_ln(oa, ob, x2, w_bf, g, b):
    t = x2.shape[0]
    return pl.pallas_call(
        _outln_kernel,
        grid=(t // TM_LN,),
        in_specs=[
            pl.BlockSpec((TM_LN, HALF_D), lambda i: (i, 0)),
            pl.BlockSpec((TM_LN, HALF_D), lambda i: (i, 0)),
            pl.BlockSpec((TM_LN, D_MODEL), lambda i: (i, 0)),
            pl.BlockSpec((D_MODEL, D_MODEL), lambda i: (0, 0)),
            pl.BlockSpec((1, D_MODEL), lambda i: (0, 0)),
            pl.BlockSpec((1, D_MODEL), lambda i: (0, 0)),
        ],
        out_specs=pl.BlockSpec((TM_LN, ROW_W), lambda i: (i, 0)),
        out_shape=jax.ShapeDtypeStruct((t, ROW_W), jnp.float32),
        compiler_params=_cparams(("parallel",), VMEM_LIMIT),
        name="out_proj_ln",
    )(oa, ob, x2, w_bf, g, b)


def _poolln_kernel(xm_ref, xp_ref, xn_ref, w_ref, sc_ref, g_ref, b_ref, o_ref, xe_ref, *, seq):
    i = pl.program_id(0)
    tm = xm_ref.shape[0]
    tiles_per_seq = seq // tm
    it = i % tiles_per_seq
    x = xm_ref[...]
    xe_ref[0:POOL_HALO, :] = jnp.where(it > 0, xp_ref[...], 0.0)
    xe_ref[POOL_HALO:POOL_HALO + tm, :] = x
    xe_ref[POOL_HALO + tm:2 * POOL_HALO + tm, :] = jnp.where(it < tiles_per_seq - 1, xn_ref[...], 0.0)
    pos = it * tm + lax.broadcasted_iota(jnp.int32, (tm, 1), 0)
    ys = []
    for gi, win in enumerate(POOL_WINDOWS):
        half = win // 2
        cs = slice(gi * POOL_CH, (gi + 1) * POOL_CH)
        tot = xe_ref[POOL_HALO - half:POOL_HALO - half + tm, cs]
        for o in range(-half + 1, half):
            tot = tot + xe_ref[POOL_HALO + o:POOL_HALO + o + tm, cs]
        cnt = (jnp.minimum(pos + half, seq) - jnp.maximum(pos - half, 0)).astype(jnp.float32)
        u = tot / cnt - x[:, cs]
        ys.append(jnp.dot(u.astype(jnp.bfloat16), w_ref[gi], preferred_element_type=jnp.float32))
    y = jnp.concatenate(ys, axis=1) * sc_ref[...]
    z = DEEPNORM_ALPHA * x + y
    o_ref[:, 0:D_MODEL] = _layer_norm(z, g_ref[...], b_ref[...])
    o_ref[:, D_MODEL:ROW_W] = jnp.zeros((tm, GATE_COLS), jnp.float32)


def _pool_ln(x2, w_bf, scale, g, b, seq):
    t = x2.shape[0]
    hb = TM_POOL // POOL_HALO
    nhb = t // POOL_HALO
    return pl.pallas_call(
        functools.partial(_poolln_kernel, seq=seq),
        grid=(t // TM_POOL,),
        in_specs=[
            pl.BlockSpec((TM_POOL, D_MODEL), lambda i: (i, 0)),
            pl.BlockSpec((POOL_HALO, D_MODEL), lambda i: (jnp.maximum(i * hb - 1, 0), 0)),
            pl.BlockSpec((POOL_HALO, D_MODEL), lambda i: (jnp.minimum((i + 1) * hb, nhb - 1), 0)),
            pl.BlockSpec(w_bf.shape, lambda i: (0, 0, 0)),
            pl.BlockSpec((1, D_MODEL), lambda i: (0, 0)),
            pl.BlockSpec((1, D_MODEL), lambda i: (0, 0)),
            pl.BlockSpec((1, D_MODEL), lambda i: (0, 0)),
        ],
        out_specs=pl.BlockSpec((TM_POOL, ROW_W), lambda i: (i, 0)),
        out_shape=jax.ShapeDtypeStruct((t, ROW_W), jnp.float32),
        scratch_shapes=[pltpu.VMEM((TM_POOL + 2 * POOL_HALO, D_MODEL), jnp.float32)],
        compiler_params=_cparams(("parallel",), VMEM_LIMIT),
        name="pool_ln",
    )(x2, x2, x2, w_bf, scale, g, b)


def _route_kernel(x_ref, wr_ref, rb_ref, gate_ref, meta_ref, cnt_ref, base_ref):
    i = pl.program_id(0)
    tr = x_ref.shape[0]

    @pl.when(i == 0)
    def _():
        base_ref[...] = jnp.zeros_like(base_ref)

    x = x_ref[:, 0:D_MODEL]
    xhi = x.astype(jnp.bfloat16)
    xlo = (x - xhi.astype(jnp.float32)).astype(jnp.bfloat16)
    l1 = lax.dot_general(wr_ref[...], xhi, _NT, preferred_element_type=jnp.float32)
    l2 = lax.dot_general(wr_ref[0:N_EXPERTS, :], xlo, _NT, preferred_element_type=jnp.float32)
    logits = l1[0:N_EXPERTS] + l1[N_EXPERTS:2 * N_EXPERTS] + l2
    aff = 1.0 / (1.0 + jnp.exp(-logits))
    sel = aff + rb_ref[...]

    best = jnp.full((1, tr), -jnp.inf, jnp.float32)
    cls = jnp.zeros((1, tr), jnp.int32)
    aa = jnp.zeros((1, tr), jnp.float32)
    ab = jnp.zeros((1, tr), jnp.float32)
    for c in range(N_CLASSES):
        ea, eb = CLASS_EA[c], CLASS_EB[c]
        ps = sel[ea:ea + 1] + sel[eb:eb + 1]
        better = ps > best
        best = jnp.where(better, ps, best)
        cls = jnp.where(better, c, cls)
        aa = jnp.where(better, aff[ea:ea + 1], aa)
        ab = jnp.where(better, aff[eb:eb + 1], ab)
    ga = aa / (aa + ab)
    gb = ab / (aa + ab)

    crow = lax.broadcasted_iota(jnp.int32, (CLASS_ROWS, tr), 0)
    onehot = (crow == cls).astype(jnp.float32)
    r_i = lax.broadcasted_iota(jnp.int32, (tr, tr), 0)
    c_i = lax.broadcasted_iota(jnp.int32, (tr, tr), 1)
    upper = (r_i < c_i).astype(jnp.bfloat16)
    excl = jnp.dot(onehot.astype(jnp.bfloat16), upper, preferred_element_type=jnp.float32)
    base = base_ref[...]
    rank = jnp.sum(onehot * (excl + base[:, 0:1]), axis=0, keepdims=True)
    base = base + jnp.sum(onehot, axis=1, keepdims=True)
    base_ref[...] = base
    cnt_ref[...] = base.astype(jnp.int32)

    mrow = lax.broadcasted_iota(jnp.int32, (8, tr), 0)
    meta_ref[...] = jnp.where(mrow == 0, cls, jnp.where(mrow == 1, rank.astype(jnp.int32), 0))
    grow = lax.broadcasted_iota(jnp.int32, (LANES, tr), 0)
    gmat = jnp.where(grow == 0, ga, jnp.where(grow == 1, gb, 0.0))
    gate_ref[...] = gmat.T


def _route(xw, wr_t, rbias):
    t = xw.shape[0]
    return pl.pallas_call(
        _route_kernel,
        grid=(t // TR_ROUTE,),
        in_specs=[
            pl.BlockSpec((TR_ROUTE, ROW_W), lambda i: (i, 0)),
            pl.BlockSpec((2 * N_EXPERTS, D_MODEL), lambda i: (0, 0)),
            pl.BlockSpec((N_EXPERTS, 1), lambda i: (0, 0)),
        ],
        out_specs=[
            pl.BlockSpec((TR_ROUTE, GATE_COLS), lambda i: (i, D_MODEL // GATE_COLS)),
            pl.BlockSpec((8, TR_ROUTE), lambda i: (0, i)),
            pl.BlockSpec((CLASS_ROWS, LANES), lambda i: (0, 0)),
        ],
        out_shape=[
            jax.ShapeDtypeStruct((t, ROW_W), jnp.float32),
            jax.ShapeDtypeStruct((8, t), jnp.int32),
            jax.ShapeDtypeStruct((CLASS_ROWS, LANES), jnp.int32),
        ],
        scratch_shapes=[pltpu.VMEM((CLASS_ROWS, LANES), jnp.float32)],
        input_output_aliases={0: 0},
        compiler_params=_cparams(("arbitrary",), VMEM_LIMIT),
        name="route",
    )(xw, wr_t, rbias)


def _row_copy(src_hbm, dst_ref, src_row, dst_row, sem):
    return pltpu.make_async_copy(src_hbm.at[pl.ds(src_row, 1)], dst_ref.at[pl.ds(dst_row, 1)], sem)


def _dispatch_kernel(dest_ref, x_ref, init_hbm, o_hbm, sem):
    del init_hbm
    i = pl.program_id(0)

    def issue(r, carry):
        _row_copy(x_ref, o_hbm, r, dest_ref[i * CH_ROWS + r], sem).start()
        return carry

    lax.fori_loop(0, CH_ROWS, issue, 0, unroll=8)
    pltpu.make_async_copy(x_ref, o_hbm.at[pl.ds(0, CH_ROWS)], sem).wait()


def _dispatch(dest, xw, p_rows):
    t = xw.shape[0]
    init = jnp.zeros((p_rows, ROW_W), jnp.float32)
    return pl.pallas_call(
        _dispatch_kernel,
        grid_spec=pltpu.PrefetchScalarGridSpec(
            num_scalar_prefetch=1,
            grid=(t // CH_ROWS,),
            in_specs=[pl.BlockSpec((CH_ROWS, ROW_W), lambda i, d: (i, 0)), pl.BlockSpec(memory_space=pl.ANY)],
            out_specs=pl.BlockSpec(memory_space=pl.ANY),
            scratch_shapes=[pltpu.SemaphoreType.DMA(())],
        ),
        out_shape=jax.ShapeDtypeStruct((p_rows, ROW_W), jnp.float32),
        input_output_aliases={2: 0},
        compiler_params=_cparams(("arbitrary",)),
        name="dispatch",
    )(dest, xw, init)


def _ffn_kernel(ta_ref, tb_ref, nv_ref, xs_ref, w1a_ref, w3a_ref, w2a_ref, w1b_ref, w3b_ref, w2b_ref, o_ref, h_ref):
    del ta_ref, tb_ref
    i = pl.program_id(0)

    @pl.when(i < nv_ref[0])
    def _():
        x = xs_ref[:, 0:D_MODEL].astype(jnp.bfloat16)
        chunk = 256
        for e, (w1_ref, w3_ref) in enumerate(((w1a_ref, w3a_ref), (w1b_ref, w3b_ref))):
            g = xs_ref[:, D_MODEL + e:D_MODEL + e + 1]
            for c in range(D_MODEL // chunk):
                cs = slice(c * chunk, (c + 1) * chunk)
                a = jnp.dot(x, w1_ref[:, cs], preferred_element_type=jnp.float32)
                bb = jnp.dot(x, w3_ref[:, cs], preferred_element_type=jnp.float32)
                hcol = (a / (1.0 + jnp.exp(-a))) * bb * g
                h_ref[:, e * D_MODEL + c * chunk:e * D_MODEL + (c + 1) * chunk] = hcol.astype(jnp.bfloat16)
        y = jnp.dot(h_ref[:, 0:D_MODEL], w2a_ref[...], preferred_element_type=jnp.float32)
        y = y + jnp.dot(h_ref[:, D_MODEL:2 * D_MODEL], w2b_ref[...], preferred_element_type=jnp.float32)
        o_ref[...] = y

    @pl.when(i >= nv_ref[0])
    def _():
        o_ref[...] = jnp.zeros_like(o_ref)


def _ffn(ta, tb, nvalid, xs, w1, w3, w2, layer):
    p_rows = xs.shape[0]
    nt = p_rows // TM_FFN

    def wspec(which):
        if which == 0:
            return pl.BlockSpec((None, None, D_MODEL, D_MODEL), lambda i, ta, tb, nv: (layer, ta[i], 0, 0))
        return pl.BlockSpec((None, None, D_MODEL, D_MODEL), lambda i, ta, tb, nv: (layer, tb[i], 0, 0))

    return pl.pallas_call(
        _ffn_kernel,
        grid_spec=pltpu.PrefetchScalarGridSpec(
            num_scalar_prefetch=3,
            grid=(nt,),
            in_specs=[pl.BlockSpec((TM_FFN, ROW_W), lambda i, ta, tb, nv: (i, 0)),
                      wspec(0), wspec(0), wspec(0), wspec(1), wspec(1), wspec(1)],
            out_specs=pl.BlockSpec((TM_FFN, D_MODEL), lambda i, ta, tb, nv: (i, 0)),
            scratch_shapes=[pltpu.VMEM((TM_FFN, 2 * D_MODEL), jnp.bfloat16)],
        ),
        out_shape=jax.ShapeDtypeStruct((p_rows, D_MODEL), jnp.float32),
        compiler_params=_cparams(("arbitrary",), VMEM_LIMIT),
        name="expert_ffn",
    )(ta, tb, nvalid, xs, w1, w3, w2, w1, w3, w2)


def _combine_kernel(dest_ref, ys_hbm, x_ref, g_ref, b_ref, o_ref, ybuf_ref, sem):
    i = pl.program_id(0)
    n = pl.num_programs(0)

    def gather(step, slot):
        def issue(r, carry):
            _row_copy(ys_hbm, ybuf_ref.at[slot], dest_ref[step * CH_ROWS + r], r, sem.at[slot]).start()
            return carry
        lax.fori_loop(0, CH_ROWS, issue, 0, unroll=8)

    @pl.when(i == 0)
    def _():
        gather(0, 0)

    @pl.when(i + 1 < n)
    def _():
        gather(i + 1, (i + 1) % 2)

    slot = i % 2
    pltpu.make_async_copy(ys_hbm.at[pl.ds(0, CH_ROWS)], ybuf_ref.at[slot], sem.at[slot]).wait()
    z = DEEPNORM_ALPHA * x_ref[:, 0:D_MODEL] + ybuf_ref[slot]
    o_ref[...] = _layer_norm(z, g_ref[...], b_ref[...])


def _combine_ln(dest, ys, xw, g, b):
    t = xw.shape[0]
    return pl.pallas_call(
        _combine_kernel,
        grid_spec=pltpu.PrefetchScalarGridSpec(
            num_scalar_prefetch=1,
            grid=(t // CH_ROWS,),
            in_specs=[pl.BlockSpec(memory_space=pl.ANY),
                      pl.BlockSpec((CH_ROWS, ROW_W), lambda i, d: (i, 0)),
                      pl.BlockSpec((1, D_MODEL), lambda i, d: (0, 0)),
                      pl.BlockSpec((1, D_MODEL), lambda i, d: (0, 0))],
            out_specs=pl.BlockSpec((CH_ROWS, D_MODEL), lambda i, d: (i, 0)),
            scratch_shapes=[pltpu.VMEM((2, CH_ROWS, D_MODEL), jnp.float32), pltpu.SemaphoreType.DMA((2,))],
        ),
        out_shape=jax.ShapeDtypeStruct((t, D_MODEL), jnp.float32),
        compiler_params=_cparams(("arbitrary",), VMEM_LIMIT),
        name="combine_ln",
    )(dest, ys, xw, g, b)


def _moe_ln(xw, wr_t, rbias, w1, w3, w2, layer, g, b):
    t = xw.shape[0]
    xw, meta, cnt = _route(xw, wr_t, rbias)
    cls, rank = meta[0], meta[1]
    counts = cnt[:N_CLASSES, 0]
    tiles_c = (counts + TM_FFN - 1) // TM_FFN
    tile_end = jnp.cumsum(tiles_c)
    row_start = (tile_end - tiles_c) * TM_FFN
    dest = row_start[cls] + rank
    nt = t // TM_FFN + N_CLASSES
    nvalid = tile_end[-1]
    tidx = jnp.arange(nt, dtype=jnp.int32)
    tcls = jnp.sum(tile_end[None, :] <= jnp.minimum(tidx, nvalid - 1)[:, None], axis=1).astype(jnp.int32)
    ta = jnp.asarray(CLASS_EA, jnp.int32)[tcls]
    tb = jnp.asarray(CLASS_EB, jnp.int32)[tcls]
    xs = _dispatch(dest, xw, nt * TM_FFN)
    ys = _ffn(ta, tb, nvalid.reshape(1).astype(jnp.int32), xs, w1, w3, w2, layer)
    return _combine_ln(dest, ys, xw, g, b)


def _rope_tables(seq):
    pos = jnp.arange(seq, dtype=jnp.float32)
    inv_freq = ROPE_THETA ** (-jnp.arange(0, HEAD_DIM, 2, dtype=jnp.float32) / HEAD_DIM)
    ang = pos[:, None] * inv_freq[None, :]
    reps = LANES // (HEAD_DIM // 2)
    return jnp.tile(jnp.cos(ang), (1, reps)), jnp.tile(jnp.sin(ang), (1, reps))


def kernel(x, w_in, w_out, rpb, pool_w, pool_scale, router_w, router_bias, moe_w1, moe_w3, moe_w2, ln_g, ln_b):
    b, s, d = x.shape
    t = b * s
    bf = jnp.bfloat16
    cos_t, sin_t = _rope_tables(s)
    rw_hi = router_w.astype(bf)
    rw_lo = (router_w - rw_hi.astype(jnp.float32)).astype(bf)
    wr_t = jnp.concatenate([rw_hi.T, rw_lo.T], axis=0)
    rbias = router_bias.astype(jnp.float32).reshape(N_EXPERTS, 1)
    w_in_bf, w_out_bf, pool_w_bf = w_in.astype(bf), w_out.astype(bf), pool_w.astype(bf)
    w1_bf, w3_bf, w2_bf = moe_w1.astype(bf), moe_w3.astype(bf), moe_w2.astype(bf)

    x2 = x.reshape(t, d)
    for layer in range(DEPTH):
        i = layer // 2
        g0, b0 = ln_g[layer, 0].reshape(1, d), ln_b[layer, 0].reshape(1, d)
        g1, b1 = ln_g[layer, 1].reshape(1, d), ln_b[layer, 1].reshape(1, d)
        if layer % 2 == 0:
            qkv = _qkv_proj(x2, w_in_bf[i], cos_t, sin_t, s).reshape(b, s, 3 * d)
            oa, la = None, None
            for _, dil in DILATED_PATTERNS:
                oa, la = _attn_a_branch(qkv, oa, la, dil)
            ob = _attn_b(qkv, _bias_table(rpb[i]))
            xw = _out_proj_ln(oa.reshape(t, HALF_D), ob.reshape(t, HALF_D), x2, w_out_bf[i], g0, b0)
        else:
            xw = _pool_ln(x2, pool_w_bf[i], pool_scale[i].reshape(1, d), g0, b0, s)
        x2 = _moe_ln(xw, wr_t, rbias, w1_bf, w3_bf, w2_bf, layer, g1, b1)
    return x2.reshape(b, s, d)
```

```python
import functools

import numpy as np
import jax
import jax.numpy as jnp
from jax import lax
from jax.experimental import pallas as pl
from jax.experimental.pallas import tpu as pltpu

D_MODEL = 1024
HEAD_DIM = 64
N_HEADS_A = 8
N_HEADS_B = 8
HALF_D = N_HEADS_A * HEAD_DIM
ROPE_THETA = 10000.0
DILATED_PATTERNS = ((128, 1), (512, 4), (2048, 16))
BAND_RADIUS = 64
GRID_W = 64
NA_ROWS = 8
NA_COLS = 16
POOL_WINDOWS = (2, 4, 8, 16)
POOL_CH = D_MODEL // len(POOL_WINDOWS)
POOL_HALO = 8
N_EXPERTS = 16
EXPERTS_PER_GROUP = 4
DEPTH = 4
DEEPNORM_ALPHA = (2.0 * DEPTH) ** 0.25
LN_EPS = 1e-5
NEG_INF = -1e30

LANES = 128
GATE_COLS = LANES
ROW_W = D_MODEL + GATE_COLS

_PAIR_CHAIN = ((0, 1), (0, 2), (1, 2), (1, 3), (0, 3), (2, 3))
CLASS_EA = tuple(g * EXPERTS_PER_GROUP + a for g in range(4) for a, _ in _PAIR_CHAIN)
CLASS_EB = tuple(g * EXPERTS_PER_GROUP + b for g in range(4) for _, b in _PAIR_CHAIN)
N_CLASSES = len(CLASS_EA)
CLASS_ROWS = 32

LT = 512
LR = 16
LJ = LT // LR

TM_QKV = 1024
TN_QKV = 512
TQ_A = 128
TB_ROWS = 8
TM_LN = 256
TM_POOL = 256
TR_ROUTE = 512
TM_FFN = 256
CH_ROWS = 256
VMEM_LIMIT = 48 * 1024 * 1024

_NT = (((1,), (1,)), ((), ()))


def _cparams(sem, vmem=None):
    return pltpu.CompilerParams(dimension_semantics=sem, vmem_limit_bytes=vmem)


def _layer_norm(z, g, b):
    mu = jnp.mean(z, axis=-1, keepdims=True)
    zc = z - mu
    var = jnp.mean(zc * zc, axis=-1, keepdims=True)
    return zc * lax.rsqrt(var + LN_EPS) * g + b


def _qkv_kernel(x_ref, w_ref, cos_ref, sin_ref, p_ref, o_ref, r_ref, xb_ref):
    n = pl.program_id(1)

    @pl.when(n == 0)
    def _():
        xb_ref[...] = x_ref[...].astype(jnp.bfloat16)

    acc = jnp.dot(xb_ref[...], w_ref[...], preferred_element_type=jnp.float32)
    scale = jnp.where(n < 2, HEAD_DIM ** -0.5, 1.0).astype(jnp.float32)

    @pl.when((n == 0) | (n == 2))
    def _():
        cos = cos_ref[...]
        sin = sin_ref[...]
        lane = lax.broadcasted_iota(jnp.int32, cos.shape, 1)
        first = (lane % HEAD_DIM) < (HEAD_DIM // 2)
        for c in range(TN_QKV // LANES):
            a = acc[:, c * LANES:(c + 1) * LANES]
            nxt = pltpu.roll(a, LANES - HEAD_DIM // 2, 1)
            prv = pltpu.roll(a, HEAD_DIM // 2, 1)
            rot = a * cos + jnp.where(first, -nxt, prv) * sin
            o_ref[:, c * LANES:(c + 1) * LANES] = (rot * scale).astype(o_ref.dtype)

    @pl.when((n != 0) & (n != 2))
    def _():
        o_ref[...] = (acc * scale).astype(o_ref.dtype)

    @pl.when(n % 2 == 0)
    def _():
        for hlf in range(TM_QKV // LT):
            rs = slice(hlf * LT, (hlf + 1) * LT)
            r_ref[rs, :] = jnp.dot(p_ref[...], o_ref[rs, :], preferred_element_type=jnp.float32).astype(r_ref.dtype)


def _residue_perm():
    p = np.zeros((LT, LT), np.float32)
    j, r = np.meshgrid(np.arange(LJ), np.arange(LR))
    p[(r * LJ + j).ravel(), (LR * j + r).ravel()] = 1.0
    return p


def _qkv_proj(x2, w_bf, cos_t, sin_t, perm, seq):
    t = x2.shape[0]
    tiles_per_seq = seq // TM_QKV
    return pl.pallas_call(
        _qkv_kernel,
        grid=(t // TM_QKV, 3 * D_MODEL // TN_QKV),
        in_specs=[
            pl.BlockSpec((TM_QKV, D_MODEL), lambda i, n: (i, 0)),
            pl.BlockSpec((D_MODEL, TN_QKV), lambda i, n: (0, n)),
            pl.BlockSpec((TM_QKV, LANES), lambda i, n: (i % tiles_per_seq, 0)),
            pl.BlockSpec((TM_QKV, LANES), lambda i, n: (i % tiles_per_seq, 0)),
            pl.BlockSpec((LT, LT), lambda i, n: (0, 0)),
        ],
        out_specs=[pl.BlockSpec((TM_QKV, TN_QKV), lambda i, n: (i, n)),
                   pl.BlockSpec((TM_QKV, HALF_D), lambda i, n: (i, n // 2))],
        out_shape=[jax.ShapeDtypeStruct((t, 3 * D_MODEL), jnp.bfloat16),
                   jax.ShapeDtypeStruct((t, 3 * HALF_D), jnp.bfloat16)],
        scratch_shapes=[pltpu.VMEM((TM_QKV, D_MODEL), jnp.bfloat16)],
        compiler_params=_cparams(("parallel", "arbitrary"), VMEM_LIMIT),
        name="qkv_proj",
    )(x2, w_bf, cos_t, sin_t, perm)


def _band_block(q, kcat, vcat, valid, o_prev, l_prev):
    tq = q.shape[0]
    lane = lax.broadcasted_iota(jnp.int32, (tq, LANES), 1)
    lo = lane < HEAD_DIM
    valid2 = jnp.concatenate([valid, valid], axis=0)
    m_tile = jnp.zeros((tq, LANES), jnp.float32)
    s_tile = jnp.ones((tq, LANES), jnp.float32)
    pair_out = []
    for hp in range(N_HEADS_A // 2):
        sl = slice(hp * LANES, (hp + 1) * LANES)
        q2, k2, v2 = q[:, sl], kcat[:, sl], vcat[:, sl]
        zero = jnp.zeros_like(q2)
        qs = jnp.concatenate([jnp.where(lo, q2, zero), jnp.where(lo, zero, q2)], axis=0)
        s = lax.dot_general(qs, k2, _NT, preferred_element_type=jnp.float32)
        s = jnp.where(valid2, s, NEG_INF)
        m = jnp.max(s, axis=1, keepdims=True)
        p = jnp.exp(s - m)
        ssum = jnp.sum(p, axis=1, keepdims=True)
        o2 = jnp.dot(p.astype(jnp.bfloat16), v2, preferred_element_type=jnp.float32)
        pair_out.append(jnp.where(lo, o2[0:tq], o2[tq:2 * tq]))
        m_tile = jnp.where(lane == 2 * hp, m[0:tq], jnp.where(lane == 2 * hp + 1, m[tq:2 * tq], m_tile))
        s_tile = jnp.where(lane == 2 * hp, ssum[0:tq], jnp.where(lane == 2 * hp + 1, ssum[tq:2 * tq], s_tile))

    lse = m_tile + jnp.log(s_tile)
    if l_prev is None:
        c_new = 1.0 / s_tile
        l_new = lse
    else:
        m2 = jnp.maximum(l_prev, lse)
        wp = jnp.exp(l_prev - m2)
        wn = jnp.exp(lse - m2)
        den = wp + wn
        l_new = m2 + jnp.log(den)
        c_prev = wp / den
        c_new = wn / (den * s_tile)
    outs = []
    for hp in range(N_HEADS_A // 2):
        sl = slice(hp * LANES, (hp + 1) * LANES)
        h0 = 2 * hp
        out = jnp.where(lo, c_new[:, h0:h0 + 1], c_new[:, h0 + 1:h0 + 2]) * pair_out[hp]
        if l_prev is not None:
            cp2 = jnp.where(lo, c_prev[:, h0:h0 + 1], c_prev[:, h0 + 1:h0 + 2])
            out = out + cp2 * o_prev[:, sl].astype(jnp.float32)
        outs.append(out)
    return outs, l_new


def _band_valid(tq, j, nblk):
    tk = tq + 2 * BAND_RADIUS
    row = lax.broadcasted_iota(jnp.int32, (tq, tk), 0)
    col = lax.broadcasted_iota(jnp.int32, (tq, tk), 1)
    rel = col - row
    valid = (rel >= 0) & (rel <= 2 * BAND_RADIUS)
    valid = valid & ((col >= BAND_RADIUS) | (j > 0))
    return valid & ((col < tq + BAND_RADIUS) | (j < nblk - 1))


def _attn_d1_kernel(q_ref, kl_ref, km_ref, kr_ref, vl_ref, vm_ref, vr_ref, op_ref, lp_ref, o_ref, l_ref, *, nblk):
    tq = q_ref.shape[0]
    valid = _band_valid(tq, pl.program_id(1), nblk)
    kcat = jnp.concatenate([kl_ref[...], km_ref[...], kr_ref[...]], axis=0)
    vcat = jnp.concatenate([vl_ref[...], vm_ref[...], vr_ref[...]], axis=0)
    outs, l_new = _band_block(q_ref[...], kcat, vcat, valid, op_ref[...], lp_ref[...])
    for hp, out in enumerate(outs):
        o_ref[:, hp * LANES:(hp + 1) * LANES] = out.astype(o_ref.dtype)
    l_ref[...] = l_new


def _attn_d16_kernel(q_ref, kl_ref, km_ref, kr_ref, vl_ref, vm_ref, vr_ref, o_ref, l_ref, *, nblk):
    def rows(ref):
        return jnp.concatenate([ref[a] for a in range(ref.shape[0])], axis=0)

    nq = q_ref.shape[0]
    tq = nq * LJ
    valid = _band_valid(tq, pl.program_id(2), nblk)
    kcat = jnp.concatenate([rows(kl_ref), rows(km_ref), rows(kr_ref)], axis=0)
    vcat = jnp.concatenate([rows(vl_ref), rows(vm_ref), rows(vr_ref)], axis=0)
    outs, l_new = _band_block(rows(q_ref), kcat, vcat, valid, None, None)
    for a in range(nq):
        for hp, out in enumerate(outs):
            o_ref[a, :, hp * LANES:(hp + 1) * LANES] = out[a * LJ:(a + 1) * LJ].astype(o_ref.dtype)
        l_ref[a] = l_new[a * LJ:(a + 1) * LJ]


def _d4_rel():
    half = LJ // 2
    sub_tile = LT // 4
    mq = np.array([4 * j + q for q in range(4) for j in range(LJ)])
    prev = np.array([4 * (half + j) + q - sub_tile for q in range(4) for j in range(half)])
    nxt = np.array([sub_tile + 4 * j + q for q in range(4) for j in range(half)])
    mk = np.concatenate([prev, mq, nxt])
    return (mk[None, :] - mq[:, None]).astype(np.int32)


def _attn_d4_kernel(q_ref, kp_ref, km_ref, kn_ref, vp_ref, vm_ref, vn_ref, op_ref, lp_ref, rel_ref, pt_ref,
                    o_ref, l_ref, ores_ref, lres_ref, *, ntiles):
    t = pl.program_id(1)
    half = LJ // 2
    nsub = LT // 4
    rel = rel_ref[...]
    col = lax.broadcasted_iota(jnp.int32, rel.shape, 1)
    valid = (rel >= -BAND_RADIUS) & (rel <= BAND_RADIUS)
    valid = valid & ((col >= 4 * half) | (t > 0)) & ((col < 4 * half + nsub) | (t < ntiles - 1))
    for r4 in range(4):
        res = [4 * q + r4 for q in range(4)]

        def main(ref):
            return jnp.concatenate([ref[r * LJ:(r + 1) * LJ, :] for r in res], axis=0)

        def window(p_ref, m_ref, n_ref):
            return jnp.concatenate([p_ref[r] for r in res] + [main(m_ref)] + [n_ref[r] for r in res], axis=0)

        outs, l_new = _band_block(main(q_ref), window(kp_ref, km_ref, kn_ref), window(vp_ref, vm_ref, vn_ref),
                                  valid, main(op_ref), main(lp_ref))
        for qi, r in enumerate(res):
            for hp, out in enumerate(outs):
                ores_ref[r * LJ:(r + 1) * LJ, hp * LANES:(hp + 1) * LANES] = (
                    out[qi * LJ:(qi + 1) * LJ].astype(ores_ref.dtype))
            lres_ref[r * LJ:(r + 1) * LJ, :] = l_new[qi * LJ:(qi + 1) * LJ]

    pt = pt_ref[...]
    o_ref[...] = jnp.dot(pt, ores_ref[...], preferred_element_type=jnp.float32).astype(o_ref.dtype)
    lres = lres_ref[...]
    l_a = lres.astype(jnp.bfloat16)
    rem = lres - l_a.astype(jnp.float32)
    l_b = rem.astype(jnp.bfloat16)
    l_c = (rem - l_b.astype(jnp.float32)).astype(jnp.bfloat16)
    l_ref[...] = (jnp.dot(pt, l_a, preferred_element_type=jnp.float32)
                  + jnp.dot(pt, l_b, preferred_element_type=jnp.float32)
                  + jnp.dot(pt, l_c, preferred_element_type=jnp.float32))


def _attn_d16(qres, b, s):
    nt = s // LT
    nq = TQ_A // LJ
    nh = BAND_RADIUS // LJ
    nblk = nt // nq
    qv = qres.reshape(b, nt, LR, LJ, 3 * HALF_D)

    def main_spec(cb, width=HALF_D):
        return pl.BlockSpec((None, nq, None, LJ, width), lambda bi, r, j: (bi, j, r, 0, cb))

    def left_spec(cb):
        return pl.BlockSpec((None, nh, None, LJ, HALF_D),
                            lambda bi, r, j: (bi, jnp.maximum(j * (nq // nh) - 1, 0), r, 0, cb))

    def right_spec(cb):
        return pl.BlockSpec((None, nh, None, LJ, HALF_D),
                            lambda bi, r, j: (bi, jnp.minimum((j + 1) * (nq // nh), nt // nh - 1), r, 0, cb))

    return pl.pallas_call(
        functools.partial(_attn_d16_kernel, nblk=nblk),
        grid=(b, LR, nblk),
        in_specs=[main_spec(0), left_spec(1), main_spec(1), right_spec(1), left_spec(2), main_spec(2), right_spec(2)],
        out_specs=[main_spec(0), main_spec(0, LANES)],
        out_shape=[jax.ShapeDtypeStruct((b, nt, LR, LJ, HALF_D), jnp.bfloat16),
                   jax.ShapeDtypeStruct((b, nt, LR, LJ, LANES), jnp.float32)],
        compiler_params=_cparams(("parallel", "parallel", "parallel"), VMEM_LIMIT),
        name="attn_dilated_16",
    )(*([qv] * 7))


def _attn_d4(qres, o_prev, l_prev, rel, perm_t, b, s):
    nt = s // LT
    half = LJ // 2
    qv = qres.reshape(b, nt, LT, 3 * HALF_D)
    qh = qres.reshape(b, nt, LR, 2, half, 3 * HALF_D)

    def tile_spec(cb, width=HALF_D):
        return pl.BlockSpec((None, None, LT, width), lambda bi, t: (bi, t, 0, cb))

    def prev_spec(cb):
        return pl.BlockSpec((None, None, LR, None, half, HALF_D),
                            lambda bi, t: (bi, jnp.maximum(t - 1, 0), 0, 1, 0, cb))

    def next_spec(cb):
        return pl.BlockSpec((None, None, LR, None, half, HALF_D),
                            lambda bi, t: (bi, jnp.minimum(t + 1, nt - 1), 0, 0, 0, cb))

    return pl.pallas_call(
        functools.partial(_attn_d4_kernel, ntiles=nt),
        grid=(b, nt),
        in_specs=[tile_spec(0), prev_spec(1), tile_spec(1), next_spec(1), prev_spec(2), tile_spec(2), next_spec(2),
                  tile_spec(0), tile_spec(0, LANES),
                  pl.BlockSpec(rel.shape, lambda bi, t: (0, 0)), pl.BlockSpec((LT, LT), lambda bi, t: (0, 0))],
        out_specs=[pl.BlockSpec((None, LT, HALF_D), lambda bi, t: (bi, t, 0)),
                   pl.BlockSpec((None, LT, LANES), lambda bi, t: (bi, t, 0))],
        out_shape=[jax.ShapeDtypeStruct((b, s, HALF_D), jnp.bfloat16),
                   jax.ShapeDtypeStruct((b, s, LANES), jnp.float32)],
        scratch_shapes=[pltpu.VMEM((LT, HALF_D), jnp.bfloat16), pltpu.VMEM((LT, LANES), jnp.float32)],
        compiler_params=_cparams(("parallel", "parallel"), VMEM_LIMIT),
        name="attn_dilated_4",
    )(qv, qh, qv, qh, qh, qv, qh, o_prev.reshape(b, nt, LT, HALF_D), l_prev.reshape(b, nt, LT, LANES), rel, perm_t)


def _attn_d1(qkv, o_prev, l_prev):
    b, s, _ = qkv.shape
    tq = TQ_A
    nblk = s // tq
    per = tq // BAND_RADIUS
    nhalo = s // BAND_RADIUS

    def main_spec(cb, width=HALF_D):
        return pl.BlockSpec((None, tq, width), lambda bi, j: (bi, j, cb))

    def left_spec(cb):
        return pl.BlockSpec((None, BAND_RADIUS, HALF_D), lambda bi, j: (bi, jnp.maximum(j * per - 1, 0), cb))

    def right_spec(cb):
        return pl.BlockSpec((None, BAND_RADIUS, HALF_D),
                            lambda bi, j: (bi, jnp.minimum((j + 1) * per, nhalo - 1), cb))

    o, _ = pl.pallas_call(
        functools.partial(_attn_d1_kernel, nblk=nblk),
        grid=(b, nblk),
        in_specs=[main_spec(0), left_spec(2), main_spec(2), right_spec(2), left_spec(4), main_spec(4), right_spec(4),
                  main_spec(0), main_spec(0, LANES)],
        out_specs=[main_spec(0), main_spec(0, LANES)],
        out_shape=[jax.ShapeDtypeStruct((b, s, HALF_D), jnp.bfloat16),
                   jax.ShapeDtypeStruct((b, s, LANES), jnp.float32)],
        compiler_params=_cparams(("parallel", "parallel"), VMEM_LIMIT),
        name="attn_dilated_1",
    )(qkv, qkv, qkv, qkv, qkv, qkv, qkv, o_prev, l_prev)
    return o


def _attn_b_kernel(q_ref, kp_ref, kc_ref, kn_ref, vp_ref, vc_ref, vn_ref, t2_ref, o_ref, kw_ref, vw_ref, *, rows):
    ib = pl.program_id(1)
    blk = TB_ROWS * GRID_W
    kw_ref[0:blk] = kp_ref[...]
    kw_ref[blk:2 * blk] = kc_ref[...]
    kw_ref[2 * blk:3 * blk] = kn_ref[...]
    vw_ref[0:blk] = vp_ref[...]
    vw_ref[blk:2 * blk] = vc_ref[...]
    vw_ref[2 * blk:3 * blk] = vn_ref[...]
    lane = lax.broadcasted_iota(jnp.int32, (GRID_W, LANES), 1)
    lo = lane < HEAD_DIM
    nkeys = NA_ROWS * GRID_W

    for hp in range(N_HEADS_B // 2):
        sl = slice(hp * LANES, (hp + 1) * LANES)

        def body(t, carry, sl=sl, hp=hp):
            i = ib * TB_ROWS + t
            rs = jnp.clip(i - NA_ROWS // 2, 0, rows - NA_ROWS)
            off = pl.multiple_of((rs - ib * TB_ROWS + TB_ROWS) * GRID_W, GRID_W)
            dlt = rs - i + (NA_ROWS - 1)
            qrow = pl.multiple_of(t * GRID_W, GRID_W)
            q2 = q_ref[pl.ds(qrow, GRID_W), sl]
            k2 = kw_ref[pl.ds(off, nkeys), sl]
            v2 = vw_ref[pl.ds(off, nkeys), sl]
            zero = jnp.zeros_like(q2)
            qs = jnp.concatenate([jnp.where(lo, q2, zero), jnp.where(lo, zero, q2)], axis=0)
            s = lax.dot_general(qs, k2, _NT, preferred_element_type=jnp.float32)
            bias = jnp.concatenate(
                [jnp.concatenate([t2_ref[hp * 2 + hh, dlt + 2 * kp] for kp in range(NA_ROWS // 2)], axis=1)
                 for hh in range(2)], axis=0)
            s = s + bias
            m = jnp.max(s, axis=1, keepdims=True)
            p = jnp.exp(s - m)
            ssum = jnp.sum(p, axis=1, keepdims=True)
            o2 = jnp.dot(p.astype(jnp.bfloat16), v2, preferred_element_type=jnp.float32) / ssum
            o_ref[pl.ds(qrow, GRID_W), sl] = jnp.where(lo, o2[0:GRID_W], o2[GRID_W:2 * GRID_W]).astype(o_ref.dtype)
            return carry

        lax.fori_loop(0, TB_ROWS, body, 0, unroll=True)


def _bias_table(rpb):
    c = np.arange(GRID_W)
    cstart = np.clip(c - NA_COLS // 2, 0, GRID_W - NA_COLS)
    colmask = (c[None, :] >= cstart[:, None]) & (c[None, :] < cstart[:, None] + NA_COLS)
    coff = np.clip(c[None, :] - c[:, None], 1 - NA_COLS, NA_COLS - 1) + (NA_COLS - 1)
    bc = rpb[:, :, coff].astype(jnp.float32)
    bc = jnp.where(colmask[None, None], bc, NEG_INF)
    return jnp.concatenate([bc[:, :-1], bc[:, 1:]], axis=-1)


def _attn_b(qkv, t2):
    b, s, _ = qkv.shape
    rows = s // GRID_W
    blk = TB_ROWS * GRID_W
    nb = s // blk

    def spec(cb, shift):
        return pl.BlockSpec((None, blk, HALF_D),
                            lambda bi, ib: (bi, jnp.clip(ib + shift, 0, nb - 1), cb))

    return pl.pallas_call(
        functools.partial(_attn_b_kernel, rows=rows),
        grid=(b, nb),
        in_specs=[spec(1, 0), spec(3, -1), spec(3, 0), spec(3, 1), spec(5, -1), spec(5, 0), spec(5, 1),
                  pl.BlockSpec(t2.shape, lambda bi, ib: (0, 0, 0, 0))],
        out_specs=pl.BlockSpec((None, blk, HALF_D), lambda bi, ib: (bi, ib, 0)),
        out_shape=jax.ShapeDtypeStruct((b, s, HALF_D), jnp.bfloat16),
        scratch_shapes=[pltpu.VMEM((3 * blk, HALF_D), jnp.bfloat16), pltpu.VMEM((3 * blk, HALF_D), jnp.bfloat16)],
        compiler_params=_cparams(("parallel", "parallel"), VMEM_LIMIT),
        name="attn_neighbourhood",
    )(qkv, qkv, qkv, qkv, qkv, qkv, qkv, t2)


def _outln_kernel(oa_ref, ob_ref, x_ref, w_ref, g_ref, b_ref, o_ref):
    h = jnp.dot(oa_ref[...], w_ref[0:HALF_D, :], preferred_element_type=jnp.float32)
    h = h + jnp.dot(ob_ref[...], w_ref[HALF_D:D_MODEL, :], preferred_element_type=jnp.float32)
    z = DEEPNORM_ALPHA * x_ref[...] + h
    o_ref[:, 0:D_MODEL] = _layer_norm(z, g_ref[...], b_ref[...])
    o_ref[:, D_MODEL:ROW_W] = jnp.zeros((o_ref.shape[0], GATE_COLS), jnp.float32)


def _out_proj_ln(oa, ob, x2, w_bf, g, b):
    t = x2.shape[0]
    return pl.pallas_call(
        _outln_kernel,
        grid=(t // TM_LN,),
        in_specs=[
            pl.BlockSpec((TM_LN, HALF_D), lambda i: (i, 0)),
            pl.BlockSpec((TM_LN, HALF_D), lambda i: (i, 0)),
            pl.BlockSpec((TM_LN, D_MODEL), lambda i: (i, 0)),
            pl.BlockSpec((D_MODEL, D_MODEL), lambda i: (0, 0)),
            pl.BlockSpec((1, D_MODEL), lambda i: (0, 0)),
            pl.BlockSpec((1, D_MODEL), lambda i: (0, 0)),
        ],
        out_specs=pl.BlockSpec((TM_LN, ROW_W), lambda i: (i, 0)),
        out_shape=jax.ShapeDtypeStruct((t, ROW_W), jnp.float32),
        compiler_params=_cparams(("parallel",), VMEM_LIMIT),
        name="out_proj_ln",
    )(oa, ob, x2, w_bf, g, b)


def _poolln_kernel(xm_ref, xp_ref, xn_ref, w_ref, sc_ref, g_ref, b_ref, o_ref, xe_ref, *, seq):
    i = pl.program_id(0)
    tm = xm_ref.shape[0]
    tiles_per_seq = seq // tm
    it = i % tiles_per_seq
    x = xm_ref[...]
    xe_ref[0:POOL_HALO, :] = jnp.where(it > 0, xp_ref[...], 0.0)
    xe_ref[POOL_HALO:POOL_HALO + tm, :] = x
    xe_ref[POOL_HALO + tm:2 * POOL_HALO + tm, :] = jnp.where(it < tiles_per_seq - 1, xn_ref[...], 0.0)
    pos = it * tm + lax.broadcasted_iota(jnp.int32, (tm, 1), 0)
    ys = []
    for gi, win in enumerate(POOL_WINDOWS):
        half = win // 2
        cs = slice(gi * POOL_CH, (gi + 1) * POOL_CH)
        tot = xe_ref[POOL_HALO - half:POOL_HALO - half + tm, cs]
        for o in range(-half + 1, half):
            tot = tot + xe_ref[POOL_HALO + o:POOL_HALO + o + tm, cs]
        cnt = (jnp.minimum(pos + half, seq) - jnp.maximum(pos - half, 0)).astype(jnp.float32)
        u = tot / cnt - x[:, cs]
        ys.append(jnp.dot(u.astype(jnp.bfloat16), w_ref[gi], preferred_element_type=jnp.float32))
    y = jnp.concatenate(ys, axis=1) * sc_ref[...]
    z = DEEPNORM_ALPHA * x + y
    o_ref[:, 0:D_MODEL] = _layer_norm(z, g_ref[...], b_ref[...])
    o_ref[:, D_MODEL:ROW_W] = jnp.zeros((tm, GATE_COLS), jnp.float32)


def _pool_ln(x2, w_bf, scale, g, b, seq):
    t = x2.shape[0]
    hb = TM_POOL // POOL_HALO
    nhb = t // POOL_HALO
    return pl.pallas_call(
        functools.partial(_poolln_kernel, seq=seq),
        grid=(t // TM_POOL,),
        in_specs=[
            pl.BlockSpec((TM_POOL, D_MODEL), lambda i: (i, 0)),
            pl.BlockSpec((POOL_HALO, D_MODEL), lambda i: (jnp.maximum(i * hb - 1, 0), 0)),
            pl.BlockSpec((POOL_HALO, D_MODEL), lambda i: (jnp.minimum((i + 1) * hb, nhb - 1), 0)),
            pl.BlockSpec(w_bf.shape, lambda i: (0, 0, 0)),
            pl.BlockSpec((1, D_MODEL), lambda i: (0, 0)),
            pl.BlockSpec((1, D_MODEL), lambda i: (0, 0)),
            pl.BlockSpec((1, D_MODEL), lambda i: (0, 0)),
        ],
        out_specs=pl.BlockSpec((TM_POOL, ROW_W), lambda i: (i, 0)),
        out_shape=jax.ShapeDtypeStruct((t, ROW_W), jnp.float32),
        scratch_shapes=[pltpu.VMEM((TM_POOL + 2 * POOL_HALO, D_MODEL), jnp.float32)],
        compiler_params=_cparams(("parallel",), VMEM_LIMIT),
        name="pool_ln",
    )(x2, x2, x2, w_bf, scale, g, b)


def _route_kernel(x_ref, wr_ref, rb_ref, gate_ref, meta_ref, cnt_ref, base_ref):
    i = pl.program_id(0)
    tr = x_ref.shape[0]

    @pl.when(i == 0)
    def _():
        base_ref[...] = jnp.zeros_like(base_ref)

    x = x_ref[:, 0:D_MODEL]
    xhi = x.astype(jnp.bfloat16)
    xlo = (x - xhi.astype(jnp.float32)).astype(jnp.bfloat16)
    l1 = lax.dot_general(wr_ref[...], xhi, _NT, preferred_element_type=jnp.float32)
    l2 = lax.dot_general(wr_ref[0:N_EXPERTS, :], xlo, _NT, preferred_element_type=jnp.float32)
    logits = l1[0:N_EXPERTS] + l1[N_EXPERTS:2 * N_EXPERTS] + l2
    aff = 1.0 / (1.0 + jnp.exp(-logits))
    sel = aff + rb_ref[...]

    best = jnp.full((1, tr), -jnp.inf, jnp.float32)
    cls = jnp.zeros((1, tr), jnp.int32)
    aa = jnp.zeros((1, tr), jnp.float32)
    ab = jnp.zeros((1, tr), jnp.float32)
    for c in range(N_CLASSES):
        ea, eb = CLASS_EA[c], CLASS_EB[c]
        ps = sel[ea:ea + 1] + sel[eb:eb + 1]
        better = ps > best
        best = jnp.where(better, ps, best)
        cls = jnp.where(better, c, cls)
        aa = jnp.where(better, aff[ea:ea + 1], aa)
        ab = jnp.where(better, aff[eb:eb + 1], ab)
    ga = aa / (aa + ab)
    gb = ab / (aa + ab)

    crow = lax.broadcasted_iota(jnp.int32, (CLASS_ROWS, tr), 0)
    onehot = (crow == cls).astype(jnp.float32)
    r_i = lax.broadcasted_iota(jnp.int32, (tr, tr), 0)
    c_i = lax.broadcasted_iota(jnp.int32, (tr, tr), 1)
    upper = (r_i < c_i).astype(jnp.bfloat16)
    excl = jnp.dot(onehot.astype(jnp.bfloat16), upper, preferred_element_type=jnp.float32)
    base = base_ref[...]
    rank = jnp.sum(onehot * (excl + base[:, 0:1]), axis=0, keepdims=True)
    base = base + jnp.sum(onehot, axis=1, keepdims=True)
    base_ref[...] = base
    cnt_ref[...] = base.astype(jnp.int32)

    mrow = lax.broadcasted_iota(jnp.int32, (8, tr), 0)
    meta_ref[...] = jnp.where(mrow == 0, cls, jnp.where(mrow == 1, rank.astype(jnp.int32), 0))
    grow = lax.broadcasted_iota(jnp.int32, (LANES, tr), 0)
    gmat = jnp.where(grow == 0, ga, jnp.where(grow == 1, gb, 0.0))
    gate_ref[...] = gmat.T


def _route(xw, wr_t, rbias):
    t = xw.shape[0]
    return pl.pallas_call(
        _route_kernel,
        grid=(t // TR_ROUTE,),
        in_specs=[
            pl.BlockSpec((TR_ROUTE, ROW_W), lambda i: (i, 0)),
            pl.BlockSpec((2 * N_EXPERTS, D_MODEL), lambda i: (0, 0)),
            pl.BlockSpec((N_EXPERTS, 1), lambda i: (0, 0)),
        ],
        out_specs=[
            pl.BlockSpec((TR_ROUTE, GATE_COLS), lambda i: (i, D_MODEL // GATE_COLS)),
            pl.BlockSpec((8, TR_ROUTE), lambda i: (0, i)),
            pl.BlockSpec((CLASS_ROWS, LANES), lambda i: (0, 0)),
        ],
        out_shape=[
            jax.ShapeDtypeStruct((t, ROW_W), jnp.float32),
            jax.ShapeDtypeStruct((8, t), jnp.int32),
            jax.ShapeDtypeStruct((CLASS_ROWS, LANES), jnp.int32),
        ],
        scratch_shapes=[pltpu.VMEM((CLASS_ROWS, LANES), jnp.float32)],
        input_output_aliases={0: 0},
        compiler_params=_cparams(("arbitrary",), VMEM_LIMIT),
        name="route",
    )(xw, wr_t, rbias)


def _row_copy(src_hbm, dst_ref, src_row, dst_row, sem):
    return pltpu.make_async_copy(src_hbm.at[pl.ds(src_row, 1)], dst_ref.at[pl.ds(dst_row, 1)], sem)


def _dispatch_kernel(dest_ref, x_ref, init_hbm, o_hbm, sem):
    del init_hbm
    i = pl.program_id(0)

    def issue(r, carry):
        _row_copy(x_ref, o_hbm, r, dest_ref[i * CH_ROWS + r], sem).start()
        return carry

    lax.fori_loop(0, CH_ROWS, issue, 0, unroll=8)
    pltpu.make_async_copy(x_ref, o_hbm.at[pl.ds(0, CH_ROWS)], sem).wait()


def _dispatch(dest, xw, p_rows):
    t = xw.shape[0]
    init = jnp.zeros((p_rows, ROW_W), jnp.float32)
    return pl.pallas_call(
        _dispatch_kernel,
        grid_spec=pltpu.PrefetchScalarGridSpec(
            num_scalar_prefetch=1,
            grid=(t // CH_ROWS,),
            in_specs=[pl.BlockSpec((CH_ROWS, ROW_W), lambda i, d: (i, 0)), pl.BlockSpec(memory_space=pl.ANY)],
            out_specs=pl.BlockSpec(memory_space=pl.ANY),
            scratch_shapes=[pltpu.SemaphoreType.DMA(())],
        ),
        out_shape=jax.ShapeDtypeStruct((p_rows, ROW_W), jnp.float32),
        input_output_aliases={2: 0},
        compiler_params=_cparams(("arbitrary",)),
        name="dispatch",
    )(dest, xw, init)


def _ffn_kernel(ta_ref, tb_ref, nv_ref, xs_ref, w1a_ref, w3a_ref, w2a_ref, w1b_ref, w3b_ref, w2b_ref, o_ref, h_ref):
    del ta_ref, tb_ref
    i = pl.program_id(0)

    @pl.when(i < nv_ref[0])
    def _():
        x = xs_ref[:, 0:D_MODEL].astype(jnp.bfloat16)
        chunk = 256
        for e, (w1_ref, w3_ref) in enumerate(((w1a_ref, w3a_ref), (w1b_ref, w3b_ref))):
            g = xs_ref[:, D_MODEL + e:D_MODEL + e + 1]
            for c in range(D_MODEL // chunk):
                cs = slice(c * chunk, (c + 1) * chunk)
                a = jnp.dot(x, w1_ref[:, cs], preferred_element_type=jnp.float32)
                bb = jnp.dot(x, w3_ref[:, cs], preferred_element_type=jnp.float32)
                hcol = (a / (1.0 + jnp.exp(-a))) * bb * g
                h_ref[:, e * D_MODEL + c * chunk:e * D_MODEL + (c + 1) * chunk] = hcol.astype(jnp.bfloat16)
        y = jnp.dot(h_ref[:, 0:D_MODEL], w2a_ref[...], preferred_element_type=jnp.float32)
        y = y + jnp.dot(h_ref[:, D_MODEL:2 * D_MODEL], w2b_ref[...], preferred_element_type=jnp.float32)
        o_ref[...] = y

    @pl.when(i >= nv_ref[0])
    def _():
        o_ref[...] = jnp.zeros_like(o_ref)


def _ffn(ta, tb, nvalid, xs, w1, w3, w2, layer):
    p_rows = xs.shape[0]
    nt = p_rows // TM_FFN

    def wspec(which):
        if which == 0:
            return pl.BlockSpec((None, None, D_MODEL, D_MODEL), lambda i, ta, tb, nv: (layer, ta[i], 0, 0))
        return pl.BlockSpec((None, None, D_MODEL, D_MODEL), lambda i, ta, tb, nv: (layer, tb[i], 0, 0))

    return pl.pallas_call(
        _ffn_kernel,
        grid_spec=pltpu.PrefetchScalarGridSpec(
            num_scalar_prefetch=3,
            grid=(nt,),
            in_specs=[pl.BlockSpec((TM_FFN, ROW_W), lambda i, ta, tb, nv: (i, 0)),
                      wspec(0), wspec(0), wspec(0), wspec(1), wspec(1), wspec(1)],
            out_specs=pl.BlockSpec((TM_FFN, D_MODEL), lambda i, ta, tb, nv: (i, 0)),
            scratch_shapes=[pltpu.VMEM((TM_FFN, 2 * D_MODEL), jnp.bfloat16)],
        ),
        out_shape=jax.ShapeDtypeStruct((p_rows, D_MODEL), jnp.float32),
        compiler_params=_cparams(("arbitrary",), VMEM_LIMIT),
        name="expert_ffn",
    )(ta, tb, nvalid, xs, w1, w3, w2, w1, w3, w2)


def _combine_kernel(dest_ref, ys_hbm, x_ref, g_ref, b_ref, o_ref, ybuf_ref, sem):
    i = pl.program_id(0)
    n = pl.num_programs(0)

    def gather(step, slot):
        def issue(r, carry):
            _row_copy(ys_hbm, ybuf_ref.at[slot], dest_ref[step * CH_ROWS + r], r, sem.at[slot]).start()
            return carry
        lax.fori_loop(0, CH_ROWS, issue, 0, unroll=8)

    @pl.when(i == 0)
    def _():
        gather(0, 0)

    @pl.when(i + 1 < n)
    def _():
        gather(i + 1, (i + 1) % 2)

    slot = i % 2
    pltpu.make_async_copy(ys_hbm.at[pl.ds(0, CH_ROWS)], ybuf_ref.at[slot], sem.at[slot]).wait()
    z = DEEPNORM_ALPHA * x_ref[:, 0:D_MODEL] + ybuf_ref[slot]
    o_ref[...] = _layer_norm(z, g_ref[...], b_ref[...])


def _combine_ln(dest, ys, xw, g, b):
    t = xw.shape[0]
    return pl.pallas_call(
        _combine_kernel,
        grid_spec=pltpu.PrefetchScalarGridSpec(
            num_scalar_prefetch=1,
            grid=(t // CH_ROWS,),
            in_specs=[pl.BlockSpec(memory_space=pl.ANY),
                      pl.BlockSpec((CH_ROWS, ROW_W), lambda i, d: (i, 0)),
                      pl.BlockSpec((1, D_MODEL), lambda i, d: (0, 0)),
                      pl.BlockSpec((1, D_MODEL), lambda i, d: (0, 0))],
            out_specs=pl.BlockSpec((CH_ROWS, D_MODEL), lambda i, d: (i, 0)),
            scratch_shapes=[pltpu.VMEM((2, CH_ROWS, D_MODEL), jnp.float32), pltpu.SemaphoreType.DMA((2,))],
        ),
        out_shape=jax.ShapeDtypeStruct((t, D_MODEL), jnp.float32),
        compiler_params=_cparams(("arbitrary",), VMEM_LIMIT),
        name="combine_ln",
    )(dest, ys, xw, g, b)


def _moe_ln(xw, wr_t, rbias, w1, w3, w2, layer, g, b):
    t = xw.shape[0]
    xw, meta, cnt = _route(xw, wr_t, rbias)
    cls, rank = meta[0], meta[1]
    counts = cnt[:N_CLASSES, 0]
    tiles_c = (counts + TM_FFN - 1) // TM_FFN
    tile_end = jnp.cumsum(tiles_c)
    row_start = (tile_end - tiles_c) * TM_FFN
    dest = row_start[cls] + rank
    nt = t // TM_FFN + N_CLASSES
    nvalid = tile_end[-1]
    tidx = jnp.arange(nt, dtype=jnp.int32)
    tcls = jnp.sum(tile_end[None, :] <= jnp.minimum(tidx, nvalid - 1)[:, None], axis=1).astype(jnp.int32)
    ta = jnp.asarray(CLASS_EA, jnp.int32)[tcls]
    tb = jnp.asarray(CLASS_EB, jnp.int32)[tcls]
    xs = _dispatch(dest, xw, nt * TM_FFN)
    ys = _ffn(ta, tb, nvalid.reshape(1).astype(jnp.int32), xs, w1, w3, w2, layer)
    return _combine_ln(dest, ys, xw, g, b)


def _rope_tables(seq):
    pos = jnp.arange(seq, dtype=jnp.float32)
    inv_freq = ROPE_THETA ** (-jnp.arange(0, HEAD_DIM, 2, dtype=jnp.float32) / HEAD_DIM)
    ang = pos[:, None] * inv_freq[None, :]
    reps = LANES // (HEAD_DIM // 2)
    return jnp.tile(jnp.cos(ang), (1, reps)), jnp.tile(jnp.sin(ang), (1, reps))


def kernel(x, w_in, w_out, rpb, pool_w, pool_scale, router_w, router_bias, moe_w1, moe_w3, moe_w2, ln_g, ln_b):
    b, s, d = x.shape
    t = b * s
    bf = jnp.bfloat16
    cos_t, sin_t = _rope_tables(s)
    perm_np = _residue_perm()
    perm, perm_t = jnp.asarray(perm_np, bf), jnp.asarray(perm_np.T, bf)
    d4_rel = jnp.asarray(_d4_rel())
    rw_hi = router_w.astype(bf)
    rw_lo = (router_w - rw_hi.astype(jnp.float32)).astype(bf)
    wr_t = jnp.concatenate([rw_hi.T, rw_lo.T], axis=0)
    rbias = router_bias.astype(jnp.float32).reshape(N_EXPERTS, 1)
    w_in_bf, w_out_bf, pool_w_bf = w_in.astype(bf), w_out.astype(bf), pool_w.astype(bf)
    w1_bf, w3_bf, w2_bf = moe_w1.astype(bf), moe_w3.astype(bf), moe_w2.astype(bf)

    x2 = x.reshape(t, d)
    for layer in range(DEPTH):
        i = layer // 2
        g0, b0 = ln_g[layer, 0].reshape(1, d), ln_b[layer, 0].reshape(1, d)
        g1, b1 = ln_g[layer, 1].reshape(1, d), ln_b[layer, 1].reshape(1, d)
        if layer % 2 == 0:
            qkv, qres = _qkv_proj(x2, w_in_bf[i], cos_t, sin_t, perm, s)
            qkv = qkv.reshape(b, s, 3 * d)
            oa, la = _attn_d16(qres, b, s)
            oa, la = _attn_d4(qres, oa, la, d4_rel, perm_t, b, s)
            oa = _attn_d1(qkv, oa, la)
            ob = _attn_b(qkv, _bias_table(rpb[i]))
            xw = _out_proj_ln(oa.reshape(t, HALF_D), ob.reshape(t, HALF_D), x2, w_out_bf[i], g0, b0)
        else:
            xw = _pool_ln(x2, pool_w_bf[i], pool_scale[i].reshape(1, d), g0, b0, s)
        x2 = _moe_ln(xw, wr_t, rbias, w1_bf, w3_bf, w2_bf, layer, g1, b1)
    return x2.reshape(b, s, d)
```

```python
import functools

import numpy as np
import jax
import jax.numpy as jnp
from jax import lax
from jax.experimental import pallas as pl
from jax.experimental.pallas import tpu as pltpu

D_MODEL = 1024
HEAD_DIM = 64
N_HEADS_A = 8
N_HEADS_B = 8
HALF_D = N_HEADS_A * HEAD_DIM
ROPE_THETA = 10000.0
DILATED_PATTERNS = ((128, 1), (512, 4), (2048, 16))
BAND_RADIUS = 64
GRID_W = 64
NA_ROWS = 8
NA_COLS = 16
POOL_WINDOWS = (2, 4, 8, 16)
POOL_CH = D_MODEL // len(POOL_WINDOWS)
POOL_HALO = 8
N_EXPERTS = 16
EXPERTS_PER_GROUP = 4
DEPTH = 4
DEEPNORM_ALPHA = (2.0 * DEPTH) ** 0.25
LN_EPS = 1e-5
NEG_INF = -1e30

LANES = 128
GATE_COLS = LANES
ROW_W = D_MODEL + GATE_COLS

_PAIR_CHAIN = ((0, 1), (0, 2), (1, 2), (1, 3), (0, 3), (2, 3))
CLASS_EA = tuple(g * EXPERTS_PER_GROUP + a for g in range(4) for a, _ in _PAIR_CHAIN)
CLASS_EB = tuple(g * EXPERTS_PER_GROUP + b for g in range(4) for _, b in _PAIR_CHAIN)
N_CLASSES = len(CLASS_EA)
CLASS_ROWS = 32

LT = 512
LR = 16
LJ = LT // LR

TM_QKV = 1024
TN_QKV = 512
TQ_A = 128
TB_ROWS = 8
TM_LN = 256
TM_POOL = 256
TR_ROUTE = 512
TM_FFN = 256
ROW_PAD = 1024
VMEM_LIMIT = 48 * 1024 * 1024

_NT = (((1,), (1,)), ((), ()))


def _cparams(sem, vmem=None):
    return pltpu.CompilerParams(dimension_semantics=sem, vmem_limit_bytes=vmem)


def _layer_norm(z, g, b):
    mu = jnp.mean(z, axis=-1, keepdims=True)
    zc = z - mu
    var = jnp.mean(zc * zc, axis=-1, keepdims=True)
    return zc * lax.rsqrt(var + LN_EPS) * g + b


def _qkv_kernel(x_ref, w_ref, cos_ref, sin_ref, p_ref, o_ref, r_ref, xb_ref):
    n = pl.program_id(1)

    @pl.when(n == 0)
    def _():
        xb_ref[...] = x_ref[...].astype(jnp.bfloat16)

    acc = jnp.dot(xb_ref[...], w_ref[...], preferred_element_type=jnp.float32)
    scale = jnp.where(n < 2, HEAD_DIM ** -0.5, 1.0).astype(jnp.float32)

    @pl.when((n == 0) | (n == 2))
    def _():
        cos = cos_ref[...]
        sin = sin_ref[...]
        lane = lax.broadcasted_iota(jnp.int32, cos.shape, 1)
        first = (lane % HEAD_DIM) < (HEAD_DIM // 2)
        for c in range(TN_QKV // LANES):
            a = acc[:, c * LANES:(c + 1) * LANES]
            nxt = pltpu.roll(a, LANES - HEAD_DIM // 2, 1)
            prv = pltpu.roll(a, HEAD_DIM // 2, 1)
            rot = a * cos + jnp.where(first, -nxt, prv) * sin
            o_ref[:, c * LANES:(c + 1) * LANES] = (rot * scale).astype(o_ref.dtype)

    @pl.when((n != 0) & (n != 2))
    def _():
        o_ref[...] = (acc * scale).astype(o_ref.dtype)

    @pl.when(n % 2 == 0)
    def _():
        for hlf in range(TM_QKV // LT):
            rs = slice(hlf * LT, (hlf + 1) * LT)
            r_ref[rs, :] = jnp.dot(p_ref[...], o_ref[rs, :], preferred_element_type=jnp.float32).astype(r_ref.dtype)


def _residue_perm():
    p = np.zeros((LT, LT), np.float32)
    j, r = np.meshgrid(np.arange(LJ), np.arange(LR))
    p[(r * LJ + j).ravel(), (LR * j + r).ravel()] = 1.0
    return p


def _qkv_proj(x2, t, w_bf, cos_t, sin_t, perm, seq):
    tiles_per_seq = seq // TM_QKV
    return pl.pallas_call(
        _qkv_kernel,
        grid=(t // TM_QKV, 3 * D_MODEL // TN_QKV),
        in_specs=[
            pl.BlockSpec((TM_QKV, D_MODEL), lambda i, n: (i, 0)),
            pl.BlockSpec((D_MODEL, TN_QKV), lambda i, n: (0, n)),
            pl.BlockSpec((TM_QKV, LANES), lambda i, n: (i % tiles_per_seq, 0)),
            pl.BlockSpec((TM_QKV, LANES), lambda i, n: (i % tiles_per_seq, 0)),
            pl.BlockSpec((LT, LT), lambda i, n: (0, 0)),
        ],
        out_specs=[pl.BlockSpec((TM_QKV, TN_QKV), lambda i, n: (i, n)),
                   pl.BlockSpec((TM_QKV, HALF_D), lambda i, n: (i, n // 2))],
        out_shape=[jax.ShapeDtypeStruct((t, 3 * D_MODEL), jnp.bfloat16),
                   jax.ShapeDtypeStruct((t, 3 * HALF_D), jnp.bfloat16)],
        scratch_shapes=[pltpu.VMEM((TM_QKV, D_MODEL), jnp.bfloat16)],
        compiler_params=_cparams(("parallel", "arbitrary"), VMEM_LIMIT),
        name="qkv_proj",
    )(x2, w_bf, cos_t, sin_t, perm)


def _band_block(q, kcat, vcat, valid, o_prev, l_prev):
    tq = q.shape[0]
    lane = lax.broadcasted_iota(jnp.int32, (tq, LANES), 1)
    lo = lane < HEAD_DIM
    valid2 = jnp.concatenate([valid, valid], axis=0)
    m_tile = jnp.zeros((tq, LANES), jnp.float32)
    s_tile = jnp.ones((tq, LANES), jnp.float32)
    pair_out = []
    for hp in range(N_HEADS_A // 2):
        sl = slice(hp * LANES, (hp + 1) * LANES)
        q2, k2, v2 = q[:, sl], kcat[:, sl], vcat[:, sl]
        zero = jnp.zeros_like(q2)
        qs = jnp.concatenate([jnp.where(lo, q2, zero), jnp.where(lo, zero, q2)], axis=0)
        s = lax.dot_general(qs, k2, _NT, preferred_element_type=jnp.float32)
        s = jnp.where(valid2, s, NEG_INF)
        m = jnp.max(s, axis=1, keepdims=True)
        p = jnp.exp(s - m)
        ssum = jnp.sum(p, axis=1, keepdims=True)
        o2 = jnp.dot(p.astype(jnp.bfloat16), v2, preferred_element_type=jnp.float32)
        pair_out.append(jnp.where(lo, o2[0:tq], o2[tq:2 * tq]))
        m_tile = jnp.where(lane == 2 * hp, m[0:tq], jnp.where(lane == 2 * hp + 1, m[tq:2 * tq], m_tile))
        s_tile = jnp.where(lane == 2 * hp, ssum[0:tq], jnp.where(lane == 2 * hp + 1, ssum[tq:2 * tq], s_tile))

    lse = m_tile + jnp.log(s_tile)
    if l_prev is None:
        c_new = 1.0 / s_tile
        l_new = lse
    else:
        m2 = jnp.maximum(l_prev, lse)
        wp = jnp.exp(l_prev - m2)
        wn = jnp.exp(lse - m2)
        den = wp + wn
        l_new = m2 + jnp.log(den)
        c_prev = wp / den
        c_new = wn / (den * s_tile)
    outs = []
    for hp in range(N_HEADS_A // 2):
        sl = slice(hp * LANES, (hp + 1) * LANES)
        h0 = 2 * hp
        out = jnp.where(lo, c_new[:, h0:h0 + 1], c_new[:, h0 + 1:h0 + 2]) * pair_out[hp]
        if l_prev is not None:
            cp2 = jnp.where(lo, c_prev[:, h0:h0 + 1], c_prev[:, h0 + 1:h0 + 2])
            out = out + cp2 * o_prev[:, sl].astype(jnp.float32)
        outs.append(out)
    return outs, l_new


def _band_valid(tq, j, nblk):
    tk = tq + 2 * BAND_RADIUS
    row = lax.broadcasted_iota(jnp.int32, (tq, tk), 0)
    col = lax.broadcasted_iota(jnp.int32, (tq, tk), 1)
    rel = col - row
    valid = (rel >= 0) & (rel <= 2 * BAND_RADIUS)
    valid = valid & ((col >= BAND_RADIUS) | (j > 0))
    return valid & ((col < tq + BAND_RADIUS) | (j < nblk - 1))


def _attn_d1_kernel(q_ref, kl_ref, km_ref, kr_ref, vl_ref, vm_ref, vr_ref, op_ref, lp_ref, o_ref, l_ref, *, nblk):
    tq = q_ref.shape[0]
    valid = _band_valid(tq, pl.program_id(1), nblk)
    kcat = jnp.concatenate([kl_ref[...], km_ref[...], kr_ref[...]], axis=0)
    vcat = jnp.concatenate([vl_ref[...], vm_ref[...], vr_ref[...]], axis=0)
    outs, l_new = _band_block(q_ref[...], kcat, vcat, valid, op_ref[...], lp_ref[...])
    for hp, out in enumerate(outs):
        o_ref[:, hp * LANES:(hp + 1) * LANES] = out.astype(o_ref.dtype)
    l_ref[...] = l_new


def _attn_d16_kernel(q_ref, kl_ref, km_ref, kr_ref, vl_ref, vm_ref, vr_ref, o_ref, l_ref, *, nblk):
    def rows(ref):
        return jnp.concatenate([ref[a] for a in range(ref.shape[0])], axis=0)

    nq = q_ref.shape[0]
    tq = nq * LJ
    valid = _band_valid(tq, pl.program_id(2), nblk)
    kcat = jnp.concatenate([rows(kl_ref), rows(km_ref), rows(kr_ref)], axis=0)
    vcat = jnp.concatenate([rows(vl_ref), rows(vm_ref), rows(vr_ref)], axis=0)
    outs, l_new = _band_block(rows(q_ref), kcat, vcat, valid, None, None)
    for a in range(nq):
        for hp, out in enumerate(outs):
            o_ref[a, :, hp * LANES:(hp + 1) * LANES] = out[a * LJ:(a + 1) * LJ].astype(o_ref.dtype)
        l_ref[a] = l_new[a * LJ:(a + 1) * LJ]


def _d4_rel():
    half = LJ // 2
    sub_tile = LT // 4
    mq = np.array([4 * j + q for q in range(4) for j in range(LJ)])
    prev = np.array([4 * (half + j) + q - sub_tile for q in range(4) for j in range(half)])
    nxt = np.array([sub_tile + 4 * j + q for q in range(4) for j in range(half)])
    mk = np.concatenate([prev, mq, nxt])
    return (mk[None, :] - mq[:, None]).astype(np.int32)


def _attn_d4_kernel(q_ref, kp_ref, km_ref, kn_ref, vp_ref, vm_ref, vn_ref, op_ref, lp_ref, rel_ref, pt_ref,
                    o_ref, l_ref, ores_ref, lres_ref, *, ntiles):
    t = pl.program_id(1)
    half = LJ // 2
    nsub = LT // 4
    rel = rel_ref[...]
    col = lax.broadcasted_iota(jnp.int32, rel.shape, 1)
    valid = (rel >= -BAND_RADIUS) & (rel <= BAND_RADIUS)
    valid = valid & ((col >= 4 * half) | (t > 0)) & ((col < 4 * half + nsub) | (t < ntiles - 1))
    for r4 in range(4):
        res = [4 * q + r4 for q in range(4)]

        def main(ref):
            return jnp.concatenate([ref[r * LJ:(r + 1) * LJ, :] for r in res], axis=0)

        def window(p_ref, m_ref, n_ref):
            return jnp.concatenate([p_ref[r] for r in res] + [main(m_ref)] + [n_ref[r] for r in res], axis=0)

        outs, l_new = _band_block(main(q_ref), window(kp_ref, km_ref, kn_ref), window(vp_ref, vm_ref, vn_ref),
                                  valid, main(op_ref), main(lp_ref))
        for qi, r in enumerate(res):
            for hp, out in enumerate(outs):
                ores_ref[r * LJ:(r + 1) * LJ, hp * LANES:(hp + 1) * LANES] = (
                    out[qi * LJ:(qi + 1) * LJ].astype(ores_ref.dtype))
            lres_ref[r * LJ:(r + 1) * LJ, :] = l_new[qi * LJ:(qi + 1) * LJ]

    pt = pt_ref[...]
    o_ref[...] = jnp.dot(pt, ores_ref[...], preferred_element_type=jnp.float32).astype(o_ref.dtype)
    lres = lres_ref[...]
    l_a = lres.astype(jnp.bfloat16)
    rem = lres - l_a.astype(jnp.float32)
    l_b = rem.astype(jnp.bfloat16)
    l_c = (rem - l_b.astype(jnp.float32)).astype(jnp.bfloat16)
    l_ref[...] = (jnp.dot(pt, l_a, preferred_element_type=jnp.float32)
                  + jnp.dot(pt, l_b, preferred_element_type=jnp.float32)
                  + jnp.dot(pt, l_c, preferred_element_type=jnp.float32))


def _attn_d16(qres, b, s):
    nt = s // LT
    nq = TQ_A // LJ
    nh = BAND_RADIUS // LJ
    nblk = nt // nq
    qv = qres.reshape(b, nt, LR, LJ, 3 * HALF_D)

    def main_spec(cb, width=HALF_D):
        return pl.BlockSpec((None, nq, None, LJ, width), lambda bi, r, j: (bi, j, r, 0, cb))

    def left_spec(cb):
        return pl.BlockSpec((None, nh, None, LJ, HALF_D),
                            lambda bi, r, j: (bi, jnp.maximum(j * (nq // nh) - 1, 0), r, 0, cb))

    def right_spec(cb):
        return pl.BlockSpec((None, nh, None, LJ, HALF_D),
                            lambda bi, r, j: (bi, jnp.minimum((j + 1) * (nq // nh), nt // nh - 1), r, 0, cb))

    return pl.pallas_call(
        functools.partial(_attn_d16_kernel, nblk=nblk),
        grid=(b, LR, nblk),
        in_specs=[main_spec(0), left_spec(1), main_spec(1), right_spec(1), left_spec(2), main_spec(2), right_spec(2)],
        out_specs=[main_spec(0), main_spec(0, LANES)],
        out_shape=[jax.ShapeDtypeStruct((b, nt, LR, LJ, HALF_D), jnp.bfloat16),
                   jax.ShapeDtypeStruct((b, nt, LR, LJ, LANES), jnp.float32)],
        compiler_params=_cparams(("parallel", "parallel", "parallel"), VMEM_LIMIT),
        name="attn_dilated_16",
    )(*([qv] * 7))


def _attn_d4(qres, o_prev, l_prev, rel, perm_t, b, s):
    nt = s // LT
    half = LJ // 2
    qv = qres.reshape(b, nt, LT, 3 * HALF_D)
    qh = qres.reshape(b, nt, LR, 2, half, 3 * HALF_D)

    def tile_spec(cb, width=HALF_D):
        return pl.BlockSpec((None, None, LT, width), lambda bi, t: (bi, t, 0, cb))

    def prev_spec(cb):
        return pl.BlockSpec((None, None, LR, None, half, HALF_D),
                            lambda bi, t: (bi, jnp.maximum(t - 1, 0), 0, 1, 0, cb))

    def next_spec(cb):
        return pl.BlockSpec((None, None, LR, None, half, HALF_D),
                            lambda bi, t: (bi, jnp.minimum(t + 1, nt - 1), 0, 0, 0, cb))

    return pl.pallas_call(
        functools.partial(_attn_d4_kernel, ntiles=nt),
        grid=(b, nt),
        in_specs=[tile_spec(0), prev_spec(1), tile_spec(1), next_spec(1), prev_spec(2), tile_spec(2), next_spec(2),
                  tile_spec(0), tile_spec(0, LANES),
                  pl.BlockSpec(rel.shape, lambda bi, t: (0, 0)), pl.BlockSpec((LT, LT), lambda bi, t: (0, 0))],
        out_specs=[pl.BlockSpec((None, LT, HALF_D), lambda bi, t: (bi, t, 0)),
                   pl.BlockSpec((None, LT, LANES), lambda bi, t: (bi, t, 0))],
        out_shape=[jax.ShapeDtypeStruct((b, s, HALF_D), jnp.bfloat16),
                   jax.ShapeDtypeStruct((b, s, LANES), jnp.float32)],
        scratch_shapes=[pltpu.VMEM((LT, HALF_D), jnp.bfloat16), pltpu.VMEM((LT, LANES), jnp.float32)],
        compiler_params=_cparams(("parallel", "parallel"), VMEM_LIMIT),
        name="attn_dilated_4",
    )(qv, qh, qv, qh, qh, qv, qh, o_prev.reshape(b, nt, LT, HALF_D), l_prev.reshape(b, nt, LT, LANES), rel, perm_t)


def _attn_d1(qkv, o_prev, l_prev):
    b, s, _ = qkv.shape
    tq = TQ_A
    nblk = s // tq
    per = tq // BAND_RADIUS
    nhalo = s // BAND_RADIUS

    def main_spec(cb, width=HALF_D):
        return pl.BlockSpec((None, tq, width), lambda bi, j: (bi, j, cb))

    def left_spec(cb):
        return pl.BlockSpec((None, BAND_RADIUS, HALF_D), lambda bi, j: (bi, jnp.maximum(j * per - 1, 0), cb))

    def right_spec(cb):
        return pl.BlockSpec((None, BAND_RADIUS, HALF_D),
                            lambda bi, j: (bi, jnp.minimum((j + 1) * per, nhalo - 1), cb))

    o, _ = pl.pallas_call(
        functools.partial(_attn_d1_kernel, nblk=nblk),
        grid=(b, nblk),
        in_specs=[main_spec(0), left_spec(2), main_spec(2), right_spec(2), left_spec(4), main_spec(4), right_spec(4),
                  main_spec(0), main_spec(0, LANES)],
        out_specs=[main_spec(0), main_spec(0, LANES)],
        out_shape=[jax.ShapeDtypeStruct((b, s, HALF_D), jnp.bfloat16),
                   jax.ShapeDtypeStruct((b, s, LANES), jnp.float32)],
        compiler_params=_cparams(("parallel", "parallel"), VMEM_LIMIT),
        name="attn_dilated_1",
    )(qkv, qkv, qkv, qkv, qkv, qkv, qkv, o_prev, l_prev)
    return o


def _attn_b_kernel(q_ref, kp_ref, kc_ref, kn_ref, vp_ref, vc_ref, vn_ref, t2_ref, o_ref, kw_ref, vw_ref, *, rows):
    ib = pl.program_id(1)
    blk = TB_ROWS * GRID_W
    kw_ref[0:blk] = kp_ref[...]
    kw_ref[blk:2 * blk] = kc_ref[...]
    kw_ref[2 * blk:3 * blk] = kn_ref[...]
    vw_ref[0:blk] = vp_ref[...]
    vw_ref[blk:2 * blk] = vc_ref[...]
    vw_ref[2 * blk:3 * blk] = vn_ref[...]
    lane = lax.broadcasted_iota(jnp.int32, (GRID_W, LANES), 1)
    lo = lane < HEAD_DIM
    nkeys = NA_ROWS * GRID_W

    for hp in range(N_HEADS_B // 2):
        sl = slice(hp * LANES, (hp + 1) * LANES)

        def body(t, carry, sl=sl, hp=hp):
            i = ib * TB_ROWS + t
            rs = jnp.clip(i - NA_ROWS // 2, 0, rows - NA_ROWS)
            off = pl.multiple_of((rs - ib * TB_ROWS + TB_ROWS) * GRID_W, GRID_W)
            dlt = rs - i + (NA_ROWS - 1)
            qrow = pl.multiple_of(t * GRID_W, GRID_W)
            q2 = q_ref[pl.ds(qrow, GRID_W), sl]
            k2 = kw_ref[pl.ds(off, nkeys), sl]
            v2 = vw_ref[pl.ds(off, nkeys), sl]
            zero = jnp.zeros_like(q2)
            qs = jnp.concatenate([jnp.where(lo, q2, zero), jnp.where(lo, zero, q2)], axis=0)
            s = lax.dot_general(qs, k2, _NT, preferred_element_type=jnp.float32)
            bias = jnp.concatenate(
                [jnp.concatenate([t2_ref[hp * 2 + hh, dlt + 2 * kp] for kp in range(NA_ROWS // 2)], axis=1)
                 for hh in range(2)], axis=0)
            s = s + bias
            m = jnp.max(s, axis=1, keepdims=True)
            p = jnp.exp(s - m)
            ssum = jnp.sum(p, axis=1, keepdims=True)
            o2 = jnp.dot(p.astype(jnp.bfloat16), v2, preferred_element_type=jnp.float32) / ssum
            o_ref[pl.ds(qrow, GRID_W), sl] = jnp.where(lo, o2[0:GRID_W], o2[GRID_W:2 * GRID_W]).astype(o_ref.dtype)
            return carry

        lax.fori_loop(0, TB_ROWS, body, 0, unroll=True)


def _bias_table(rpb):
    c = np.arange(GRID_W)
    cstart = np.clip(c - NA_COLS // 2, 0, GRID_W - NA_COLS)
    colmask = (c[None, :] >= cstart[:, None]) & (c[None, :] < cstart[:, None] + NA_COLS)
    coff = np.clip(c[None, :] - c[:, None], 1 - NA_COLS, NA_COLS - 1) + (NA_COLS - 1)
    bc = rpb[:, :, coff].astype(jnp.float32)
    bc = jnp.where(colmask[None, None], bc, NEG_INF)
    return jnp.concatenate([bc[:, :-1], bc[:, 1:]], axis=-1)


def _attn_b(qkv, t2):
    b, s, _ = qkv.shape
    rows = s // GRID_W
    blk = TB_ROWS * GRID_W
    nb = s // blk

    def spec(cb, shift):
        return pl.BlockSpec((None, blk, HALF_D),
                            lambda bi, ib: (bi, jnp.clip(ib + shift, 0, nb - 1), cb))

    return pl.pallas_call(
        functools.partial(_attn_b_kernel, rows=rows),
        grid=(b, nb),
        in_specs=[spec(1, 0), spec(3, -1), spec(3, 0), spec(3, 1), spec(5, -1), spec(5, 0), spec(5, 1),
                  pl.BlockSpec(t2.shape, lambda bi, ib: (0, 0, 0, 0))],
        out_specs=pl.BlockSpec((None, blk, HALF_D), lambda bi, ib: (bi, ib, 0)),
        out_shape=jax.ShapeDtypeStruct((b, s, HALF_D), jnp.bfloat16),
        scratch_shapes=[pltpu.VMEM((3 * blk, HALF_D), jnp.bfloat16), pltpu.VMEM((3 * blk, HALF_D), jnp.bfloat16)],
        compiler_params=_cparams(("parallel", "parallel"), VMEM_LIMIT),
        name="attn_neighbourhood",
    )(qkv, qkv, qkv, qkv, qkv, qkv, qkv, t2)


def _outln_kernel(oa_ref, ob_ref, x_ref, w_ref, g_ref, b_ref, o_ref):
    h = jnp.dot(oa_ref[...], w_ref[0:HALF_D, :], preferred_element_type=jnp.float32)
    h = h + jnp.dot(ob_ref[...], w_ref[HALF_D:D_MODEL, :], preferred_element_type=jnp.float32)
    z = DEEPNORM_ALPHA * x_ref[...] + h
    o_ref[:, 0:D_MODEL] = _layer_norm(z, g_ref[...], b_ref[...])
    o_ref[:, D_MODEL:ROW_W] = jnp.zeros((o_ref.shape[0], GATE_COLS), jnp.float32)


def _out_proj_ln(oa, ob, x2, t, w_bf, g, b):
    return pl.pallas_call(
        _outln_kernel,
        grid=(t // TM_LN,),
        in_specs=[
            pl.BlockSpec((TM_LN, HALF_D), lambda i: (i, 0)),
            pl.BlockSpec((TM_LN, HALF_D), lambda i: (i, 0)),
            pl.BlockSpec((TM_LN, D_MODEL), lambda i: (i, 0)),
            pl.BlockSpec((D_MODEL, D_MODEL), lambda i: (0, 0)),
            pl.BlockSpec((1, D_MODEL), lambda i: (0, 0)),
            pl.BlockSpec((1, D_MODEL), lambda i: (0, 0)),
        ],
        out_specs=pl.BlockSpec((TM_LN, ROW_W), lambda i: (i, 0)),
        out_shape=jax.ShapeDtypeStruct((t, ROW_W), jnp.float32),
        compiler_params=_cparams(("parallel",), VMEM_LIMIT),
        name="out_proj_ln",
    )(oa, ob, x2, w_bf, g, b)


def _poolln_kernel(xm_ref, xp_ref, xn_ref, w_ref, sc_ref, g_ref, b_ref, o_ref, xe_ref, *, seq):
    i = pl.program_id(0)
    tm = xm_ref.shape[0]
    tiles_per_seq = seq // tm
    it = i % tiles_per_seq
    x = xm_ref[...]
    xe_ref[0:POOL_HALO, :] = jnp.where(it > 0, xp_ref[...], 0.0)
    xe_ref[POOL_HALO:POOL_HALO + tm, :] = x
    xe_ref[POOL_HALO + tm:2 * POOL_HALO + tm, :] = jnp.where(it < tiles_per_seq - 1, xn_ref[...], 0.0)
    pos = it * tm + lax.broadcasted_iota(jnp.int32, (tm, 1), 0)
    ys = []
    for gi, win in enumerate(POOL_WINDOWS):
        half = win // 2
        cs = slice(gi * POOL_CH, (gi + 1) * POOL_CH)
        tot = xe_ref[POOL_HALO - half:POOL_HALO - half + tm, cs]
        for o in range(-half + 1, half):
            tot = tot + xe_ref[POOL_HALO + o:POOL_HALO + o + tm, cs]
        cnt = (jnp.minimum(pos + half, seq) - jnp.maximum(pos - half, 0)).astype(jnp.float32)
        u = tot / cnt - x[:, cs]
        ys.append(jnp.dot(u.astype(jnp.bfloat16), w_ref[gi], preferred_element_type=jnp.float32))
    y = jnp.concatenate(ys, axis=1) * sc_ref[...]
    z = DEEPNORM_ALPHA * x + y
    o_ref[:, 0:D_MODEL] = _layer_norm(z, g_ref[...], b_ref[...])
    o_ref[:, D_MODEL:ROW_W] = jnp.zeros((tm, GATE_COLS), jnp.float32)


def _pool_ln(x2, t, w_bf, scale, g, b, seq):
    hb = TM_POOL // POOL_HALO
    nhb = t // POOL_HALO
    return pl.pallas_call(
        functools.partial(_poolln_kernel, seq=seq),
        grid=(t // TM_POOL,),
        in_specs=[
            pl.BlockSpec((TM_POOL, D_MODEL), lambda i: (i, 0)),
            pl.BlockSpec((POOL_HALO, D_MODEL), lambda i: (jnp.maximum(i * hb - 1, 0), 0)),
            pl.BlockSpec((POOL_HALO, D_MODEL), lambda i: (jnp.minimum((i + 1) * hb, nhb - 1), 0)),
            pl.BlockSpec(w_bf.shape, lambda i: (0, 0, 0)),
            pl.BlockSpec((1, D_MODEL), lambda i: (0, 0)),
            pl.BlockSpec((1, D_MODEL), lambda i: (0, 0)),
            pl.BlockSpec((1, D_MODEL), lambda i: (0, 0)),
        ],
        out_specs=pl.BlockSpec((TM_POOL, ROW_W), lambda i: (i, 0)),
        out_shape=jax.ShapeDtypeStruct((t, ROW_W), jnp.float32),
        scratch_shapes=[pltpu.VMEM((TM_POOL + 2 * POOL_HALO, D_MODEL), jnp.float32)],
        compiler_params=_cparams(("parallel",), VMEM_LIMIT),
        name="pool_ln",
    )(x2, x2, x2, w_bf, scale, g, b)


def _route_kernel(x_ref, wr_ref, rb_ref, gate_ref, meta_ref, cnt_ref, base_ref):
    i = pl.program_id(0)
    tr = x_ref.shape[0]

    @pl.when(i == 0)
    def _():
        base_ref[...] = jnp.zeros_like(base_ref)

    x = x_ref[:, 0:D_MODEL]
    xhi = x.astype(jnp.bfloat16)
    xlo = (x - xhi.astype(jnp.float32)).astype(jnp.bfloat16)
    l1 = lax.dot_general(wr_ref[...], xhi, _NT, preferred_element_type=jnp.float32)
    l2 = lax.dot_general(wr_ref[0:N_EXPERTS, :], xlo, _NT, preferred_element_type=jnp.float32)
    logits = l1[0:N_EXPERTS] + l1[N_EXPERTS:2 * N_EXPERTS] + l2
    aff = 1.0 / (1.0 + jnp.exp(-logits))
    sel = aff + rb_ref[...]

    best = jnp.full((1, tr), -jnp.inf, jnp.float32)
    cls = jnp.zeros((1, tr), jnp.int32)
    aa = jnp.zeros((1, tr), jnp.float32)
    ab = jnp.zeros((1, tr), jnp.float32)
    for c in range(N_CLASSES):
        ea, eb = CLASS_EA[c], CLASS_EB[c]
        ps = sel[ea:ea + 1] + sel[eb:eb + 1]
        better = ps > best
        best = jnp.where(better, ps, best)
        cls = jnp.where(better, c, cls)
        aa = jnp.where(better, aff[ea:ea + 1], aa)
        ab = jnp.where(better, aff[eb:eb + 1], ab)
    ga = aa / (aa + ab)
    gb = ab / (aa + ab)

    crow = lax.broadcasted_iota(jnp.int32, (CLASS_ROWS, tr), 0)
    onehot = (crow == cls).astype(jnp.float32)
    r_i = lax.broadcasted_iota(jnp.int32, (tr, tr), 0)
    c_i = lax.broadcasted_iota(jnp.int32, (tr, tr), 1)
    upper = (r_i < c_i).astype(jnp.bfloat16)
    excl = jnp.dot(onehot.astype(jnp.bfloat16), upper, preferred_element_type=jnp.float32)
    base = base_ref[...]
    rank = jnp.sum(onehot * (excl + base[:, 0:1]), axis=0, keepdims=True)
    base = base + jnp.sum(onehot, axis=1, keepdims=True)
    base_ref[...] = base
    cnt_ref[...] = base.astype(jnp.int32)

    mrow = lax.broadcasted_iota(jnp.int32, (8, tr), 0)
    meta_ref[...] = jnp.where(mrow == 0, cls, jnp.where(mrow == 1, rank.astype(jnp.int32), 0))
    grow = lax.broadcasted_iota(jnp.int32, (LANES, tr), 0)
    gmat = jnp.where(grow == 0, ga, jnp.where(grow == 1, gb, 0.0))
    gate_ref[...] = gmat.T


def _route(xw, wr_t, rbias):
    t = xw.shape[0]
    return pl.pallas_call(
        _route_kernel,
        grid=(t // TR_ROUTE,),
        in_specs=[
            pl.BlockSpec((TR_ROUTE, ROW_W), lambda i: (i, 0)),
            pl.BlockSpec((2 * N_EXPERTS, D_MODEL), lambda i: (0, 0)),
            pl.BlockSpec((N_EXPERTS, 1), lambda i: (0, 0)),
        ],
        out_specs=[
            pl.BlockSpec((TR_ROUTE, GATE_COLS), lambda i: (i, D_MODEL // GATE_COLS)),
            pl.BlockSpec((8, TR_ROUTE), lambda i: (0, i)),
            pl.BlockSpec((CLASS_ROWS, LANES), lambda i: (0, 0)),
        ],
        out_shape=[
            jax.ShapeDtypeStruct((t, ROW_W), jnp.float32),
            jax.ShapeDtypeStruct((8, t), jnp.int32),
            jax.ShapeDtypeStruct((CLASS_ROWS, LANES), jnp.int32),
        ],
        scratch_shapes=[pltpu.VMEM((CLASS_ROWS, LANES), jnp.float32)],
        input_output_aliases={0: 0},
        compiler_params=_cparams(("arbitrary",), VMEM_LIMIT),
        name="route",
    )(xw, wr_t, rbias)


def _row_copy(src_hbm, dst_ref, src_row, dst_row, sem):
    return pltpu.make_async_copy(src_hbm.at[pl.ds(src_row, 1)], dst_ref.at[pl.ds(dst_row, 1)], sem)


def _moe_ffn_kernel(src_ref, ta_ref, tb_ref, nv_ref, xw_hbm, w1a_ref, w3a_ref, w2a_ref, w1b_ref, w3b_ref, w2b_ref,
                    g_ref, b_ref, o_hbm, xbuf, obuf, h_ref, gsem, ssem, *, t_rows, n_tiles):
    del ta_ref, tb_ref
    i = pl.program_id(0)
    nv = nv_ref[0]
    tm = TM_FFN
    chunk = 256
    n_chunks = 2 * D_MODEL // chunk
    rows_per_chunk = tm // n_chunks

    def gather_copy(tile, r, slot):
        tok = jnp.maximum(src_ref[tile * tm + r], 0)
        return _row_copy(xw_hbm, xbuf.at[slot], tok, r, gsem.at[slot])

    def scatter_copy(tile, r, slot, discard):
        s = src_ref[tile * tm + r]
        tok = jnp.where((s < 0) | discard, t_rows + slot * tm + r, s)
        return _row_copy(obuf.at[slot], o_hbm, r, tok, ssem.at[slot])

    def wait_gather(slot):
        pltpu.make_async_copy(xw_hbm.at[pl.ds(0, tm)], xbuf.at[slot], gsem.at[slot]).wait()

    def wait_scatter(slot):
        pltpu.make_async_copy(obuf.at[slot], o_hbm.at[pl.ds(0, tm)], ssem.at[slot]).wait()

    @pl.when(i == 0)
    def _():
        obuf[...] = jnp.zeros_like(obuf)
        spare = [pltpu.make_async_copy(obuf.at[k % 2], o_hbm.at[pl.ds(t_rows + k * tm, tm)], ssem.at[0])
                 for k in range(ROW_PAD // tm)]
        for cp in spare:
            cp.start()
        for cp in spare:
            cp.wait()

        def issue(r, carry):
            gather_copy(0, r, 0).start()
            return carry
        lax.fori_loop(0, tm, issue, 0, unroll=8)

    @pl.when(i < nv)
    def _():
        slot = i % 2
        other = 1 - slot
        nxt = jnp.minimum(i + 1, n_tiles - 1)
        prv = jnp.maximum(i - 1, 0)
        first = i == 0
        wait_gather(slot)
        x = xbuf[slot, :, 0:D_MODEL].astype(jnp.bfloat16)
        ci = 0
        for e, (w1_ref, w3_ref) in enumerate(((w1a_ref, w3a_ref), (w1b_ref, w3b_ref))):
            g = xbuf[slot, :, D_MODEL + e:D_MODEL + e + 1]
            for c in range(D_MODEL // chunk):
                cs = slice(c * chunk, (c + 1) * chunk)
                a = jnp.dot(x, w1_ref[:, cs], preferred_element_type=jnp.float32)
                bb = jnp.dot(x, w3_ref[:, cs], preferred_element_type=jnp.float32)
                hcol = (a / (1.0 + jnp.exp(-a))) * bb * g
                h_ref[:, e * D_MODEL + c * chunk:e * D_MODEL + (c + 1) * chunk] = hcol.astype(jnp.bfloat16)
                for r in range(ci * rows_per_chunk, (ci + 1) * rows_per_chunk):
                    gather_copy(nxt, r, other).start()
                    scatter_copy(prv, r, other, first).start()
                ci += 1
        y = jnp.dot(h_ref[:, 0:D_MODEL], w2a_ref[...], preferred_element_type=jnp.float32)
        y = y + jnp.dot(h_ref[:, D_MODEL:2 * D_MODEL], w2b_ref[...], preferred_element_type=jnp.float32)
        z = DEEPNORM_ALPHA * xbuf[slot, :, 0:D_MODEL] + y
        out = _layer_norm(z, g_ref[...], b_ref[...])

        @pl.when(i >= 1)
        def _():
            wait_scatter(slot)

        obuf[slot] = out

        @pl.when(i == nv - 1)
        def _():
            def issue(r, carry):
                scatter_copy(i, r, slot, False).start()
                return carry
            lax.fori_loop(0, tm, issue, 0, unroll=8)
            wait_scatter(other)
            wait_scatter(slot)
            wait_gather(other)


def _moe_ffn(src, ta, tb, nvalid, xw, w1, w3, w2, layer, g, b, t_rows):
    nt = src.shape[0] // TM_FFN

    def wspec(which):
        if which == 0:
            return pl.BlockSpec((None, None, D_MODEL, D_MODEL), lambda i, s, ta, tb, nv: (layer, ta[i], 0, 0))
        return pl.BlockSpec((None, None, D_MODEL, D_MODEL), lambda i, s, ta, tb, nv: (layer, tb[i], 0, 0))

    vec = pl.BlockSpec((1, D_MODEL), lambda i, s, ta, tb, nv: (0, 0))
    return pl.pallas_call(
        functools.partial(_moe_ffn_kernel, t_rows=t_rows, n_tiles=nt),
        grid_spec=pltpu.PrefetchScalarGridSpec(
            num_scalar_prefetch=4,
            grid=(nt,),
            in_specs=[pl.BlockSpec(memory_space=pl.ANY),
                      wspec(0), wspec(0), wspec(0), wspec(1), wspec(1), wspec(1), vec, vec],
            out_specs=pl.BlockSpec(memory_space=pl.ANY),
            scratch_shapes=[pltpu.VMEM((2, TM_FFN, ROW_W), jnp.float32),
                            pltpu.VMEM((2, TM_FFN, D_MODEL), jnp.float32),
                            pltpu.VMEM((TM_FFN, 2 * D_MODEL), jnp.bfloat16),
                            pltpu.SemaphoreType.DMA((2,)), pltpu.SemaphoreType.DMA((2,))],
        ),
        out_shape=jax.ShapeDtypeStruct((t_rows + ROW_PAD, D_MODEL), jnp.float32),
        compiler_params=_cparams(("arbitrary",), VMEM_LIMIT),
        name="moe_ffn",
    )(src, ta, tb, nvalid, xw, w1, w3, w2, w1, w3, w2, g, b)


def _moe_ln(xw, wr_t, rbias, w1, w3, w2, layer, g, b):
    t = xw.shape[0]
    xw, meta, cnt = _route(xw, wr_t, rbias)
    cls, rank = meta[0], meta[1]
    counts = cnt[:N_CLASSES, 0]
    tiles_c = (counts + TM_FFN - 1) // TM_FFN
    tile_end = jnp.cumsum(tiles_c)
    row_start = (tile_end - tiles_c) * TM_FFN
    dest = row_start[cls] + rank
    nt = t // TM_FFN + N_CLASSES
    nvalid = tile_end[-1]
    tidx = jnp.arange(nt, dtype=jnp.int32)
    tcls = jnp.sum(tile_end[None, :] <= jnp.minimum(tidx, nvalid - 1)[:, None], axis=1).astype(jnp.int32)
    ta = jnp.asarray(CLASS_EA, jnp.int32)[tcls]
    tb = jnp.asarray(CLASS_EB, jnp.int32)[tcls]
    src = jnp.full((nt * TM_FFN,), -1, jnp.int32).at[dest].set(jnp.arange(t, dtype=jnp.int32))
    return _moe_ffn(src, ta, tb, nvalid.reshape(1).astype(jnp.int32), xw, w1, w3, w2, layer, g, b, t)


def _rope_tables(seq):
    pos = jnp.arange(seq, dtype=jnp.float32)
    inv_freq = ROPE_THETA ** (-jnp.arange(0, HEAD_DIM, 2, dtype=jnp.float32) / HEAD_DIM)
    ang = pos[:, None] * inv_freq[None, :]
    reps = LANES // (HEAD_DIM // 2)
    return jnp.tile(jnp.cos(ang), (1, reps)), jnp.tile(jnp.sin(ang), (1, reps))


def kernel(x, w_in, w_out, rpb, pool_w, pool_scale, router_w, router_bias, moe_w1, moe_w3, moe_w2, ln_g, ln_b):
    b, s, d = x.shape
    t = b * s
    bf = jnp.bfloat16
    cos_t, sin_t = _rope_tables(s)
    perm_np = _residue_perm()
    perm, perm_t = jnp.asarray(perm_np, bf), jnp.asarray(perm_np.T, bf)
    d4_rel = jnp.asarray(_d4_rel())
    rw_hi = router_w.astype(bf)
    rw_lo = (router_w - rw_hi.astype(jnp.float32)).astype(bf)
    wr_t = jnp.concatenate([rw_hi.T, rw_lo.T], axis=0)
    rbias = router_bias.astype(jnp.float32).reshape(N_EXPERTS, 1)
    w_in_bf, w_out_bf, pool_w_bf = w_in.astype(bf), w_out.astype(bf), pool_w.astype(bf)
    w1_bf, w3_bf, w2_bf = moe_w1.astype(bf), moe_w3.astype(bf), moe_w2.astype(bf)

    x2 = x.reshape(t, d)
    for layer in range(DEPTH):
        i = layer // 2
        g0, b0 = ln_g[layer, 0].reshape(1, d), ln_b[layer, 0].reshape(1, d)
        g1, b1 = ln_g[layer, 1].reshape(1, d), ln_b[layer, 1].reshape(1, d)
        if layer % 2 == 0:
            qkv, qres = _qkv_proj(x2, t, w_in_bf[i], cos_t, sin_t, perm, s)
            qkv = qkv.reshape(b, s, 3 * d)
            oa, la = _attn_d16(qres, b, s)
            oa, la = _attn_d4(qres, oa, la, d4_rel, perm_t, b, s)
            oa = _attn_d1(qkv, oa, la)
            ob = _attn_b(qkv, _bias_table(rpb[i]))
            xw = _out_proj_ln(oa.reshape(t, HALF_D), ob.reshape(t, HALF_D), x2, t, w_out_bf[i], g0, b0)
        else:
            xw = _pool_ln(x2, t, pool_w_bf[i], pool_scale[i].reshape(1, d), g0, b0, s)
        x2 = _moe_ln(xw, wr_t, rbias, w1_bf, w3_bf, w2_bf, layer, g1, b1)
    return x2[:t].reshape(b, s, d)
```

```python
import functools

import numpy as np
import jax
import jax.numpy as jnp
from jax import lax
from jax.experimental import pallas as pl
from jax.experimental.pallas import tpu as pltpu

D_MODEL = 1024
HEAD_DIM = 64
N_HEADS_A = 8
N_HEADS_B = 8
HALF_D = N_HEADS_A * HEAD_DIM
ROPE_THETA = 10000.0
DILATED_PATTERNS = ((128, 1), (512, 4), (2048, 16))
BAND_RADIUS = 64
GRID_W = 64
NA_ROWS = 8
NA_COLS = 16
POOL_WINDOWS = (2, 4, 8, 16)
POOL_CH = D_MODEL // len(POOL_WINDOWS)
POOL_HALO = 8
N_EXPERTS = 16
EXPERTS_PER_GROUP = 4
DEPTH = 4
DEEPNORM_ALPHA = (2.0 * DEPTH) ** 0.25
LN_EPS = 1e-5
NEG_INF = -1e30

LANES = 128
GATE_COLS = LANES
ROW_W = D_MODEL + GATE_COLS

_PAIR_CHAIN = ((0, 1), (0, 2), (1, 2), (1, 3), (0, 3), (2, 3))
CLASS_EA = tuple(g * EXPERTS_PER_GROUP + a for g in range(4) for a, _ in _PAIR_CHAIN)
CLASS_EB = tuple(g * EXPERTS_PER_GROUP + b for g in range(4) for _, b in _PAIR_CHAIN)
N_CLASSES = len(CLASS_EA)
CLASS_ROWS = 32

LT = 512
LR = 16
LJ = LT // LR

TM_QKV = 1024
TN_QKV = 512
TQ_A = 256
TB_ROWS = 8
TM_LN = 256
TM_POOL = 256
TM_FFN = 256
ROW_PAD = 1024
VMEM_LIMIT = 48 * 1024 * 1024

_NT = (((1,), (1,)), ((), ()))


def _cparams(sem, vmem=None):
    return pltpu.CompilerParams(dimension_semantics=sem, vmem_limit_bytes=vmem)


def _layer_norm(z, g, b):
    mu = jnp.mean(z, axis=-1, keepdims=True)
    zc = z - mu
    var = jnp.mean(zc * zc, axis=-1, keepdims=True)
    return zc * lax.rsqrt(var + LN_EPS) * g + b


def _qkv_kernel(x_ref, w_ref, cos_ref, sin_ref, p_ref, o_ref, r_ref, xb_ref):
    n = pl.program_id(1)

    @pl.when(n == 0)
    def _():
        xb_ref[...] = x_ref[...].astype(jnp.bfloat16)

    acc = jnp.dot(xb_ref[...], w_ref[...], preferred_element_type=jnp.float32)
    scale = jnp.where(n < 2, HEAD_DIM ** -0.5, 1.0).astype(jnp.float32)

    @pl.when((n == 0) | (n == 2))
    def _():
        cos = cos_ref[...]
        sin = sin_ref[...]
        lane = lax.broadcasted_iota(jnp.int32, cos.shape, 1)
        first = (lane % HEAD_DIM) < (HEAD_DIM // 2)
        for c in range(TN_QKV // LANES):
            a = acc[:, c * LANES:(c + 1) * LANES]
            nxt = pltpu.roll(a, LANES - HEAD_DIM // 2, 1)
            prv = pltpu.roll(a, HEAD_DIM // 2, 1)
            rot = a * cos + jnp.where(first, -nxt, prv) * sin
            o_ref[:, c * LANES:(c + 1) * LANES] = (rot * scale).astype(o_ref.dtype)

    @pl.when((n != 0) & (n != 2))
    def _():
        o_ref[...] = (acc * scale).astype(o_ref.dtype)

    @pl.when(n % 2 == 0)
    def _():
        for hlf in range(TM_QKV // LT):
            rs = slice(hlf * LT, (hlf + 1) * LT)
            r_ref[rs, :] = jnp.dot(p_ref[...], o_ref[rs, :], preferred_element_type=jnp.float32).astype(r_ref.dtype)


def _residue_perm():
    p = np.zeros((LT, LT), np.float32)
    j, r = np.meshgrid(np.arange(LJ), np.arange(LR))
    p[(r * LJ + j).ravel(), (LR * j + r).ravel()] = 1.0
    return p


def _qkv_proj(x2, t, w_bf, cos_t, sin_t, perm, seq):
    tiles_per_seq = seq // TM_QKV
    return pl.pallas_call(
        _qkv_kernel,
        grid=(t // TM_QKV, 3 * D_MODEL // TN_QKV),
        in_specs=[
            pl.BlockSpec((TM_QKV, D_MODEL), lambda i, n: (i, 0)),
            pl.BlockSpec((D_MODEL, TN_QKV), lambda i, n: (0, n)),
            pl.BlockSpec((TM_QKV, LANES), lambda i, n: (i % tiles_per_seq, 0)),
            pl.BlockSpec((TM_QKV, LANES), lambda i, n: (i % tiles_per_seq, 0)),
            pl.BlockSpec((LT, LT), lambda i, n: (0, 0)),
        ],
        out_specs=[pl.BlockSpec((TM_QKV, TN_QKV), lambda i, n: (i, n)),
                   pl.BlockSpec((TM_QKV, HALF_D), lambda i, n: (i, n // 2))],
        out_shape=[jax.ShapeDtypeStruct((t, 3 * D_MODEL), jnp.bfloat16),
                   jax.ShapeDtypeStruct((t, 3 * HALF_D), jnp.bfloat16)],
        scratch_shapes=[pltpu.VMEM((TM_QKV, D_MODEL), jnp.bfloat16)],
        compiler_params=_cparams(("parallel", "arbitrary"), VMEM_LIMIT),
        name="qkv_proj",
    )(x2, w_bf, cos_t, sin_t, perm)


def _band_block(q, kcat, vcat, valid, o_prev, l_prev):
    tq = q.shape[0]
    lane = lax.broadcasted_iota(jnp.int32, (tq, LANES), 1)
    lo = lane < HEAD_DIM
    valid2 = jnp.concatenate([valid, valid], axis=0)
    m_tile = jnp.zeros((tq, LANES), jnp.float32)
    s_tile = jnp.ones((tq, LANES), jnp.float32)
    pair_out = []
    for hp in range(N_HEADS_A // 2):
        sl = slice(hp * LANES, (hp + 1) * LANES)
        q2, k2, v2 = q[:, sl], kcat[:, sl], vcat[:, sl]
        zero = jnp.zeros_like(q2)
        qs = jnp.concatenate([jnp.where(lo, q2, zero), jnp.where(lo, zero, q2)], axis=0)
        s = lax.dot_general(qs, k2, _NT, preferred_element_type=jnp.float32)
        s = jnp.where(valid2, s, NEG_INF)
        m = jnp.max(s, axis=1, keepdims=True)
        p = jnp.exp(s - m)
        ssum = jnp.sum(p, axis=1, keepdims=True)
        o2 = jnp.dot(p.astype(jnp.bfloat16), v2, preferred_element_type=jnp.float32)
        pair_out.append(jnp.where(lo, o2[0:tq], o2[tq:2 * tq]))
        m_tile = jnp.where(lane == 2 * hp, m[0:tq], jnp.where(lane == 2 * hp + 1, m[tq:2 * tq], m_tile))
        s_tile = jnp.where(lane == 2 * hp, ssum[0:tq], jnp.where(lane == 2 * hp + 1, ssum[tq:2 * tq], s_tile))

    lse = m_tile + jnp.log(s_tile)
    if l_prev is None:
        c_new = 1.0 / s_tile
        l_new = lse
    else:
        m2 = jnp.maximum(l_prev, lse)
        wp = jnp.exp(l_prev - m2)
        wn = jnp.exp(lse - m2)
        den = wp + wn
        l_new = m2 + jnp.log(den)
        c_prev = wp / den
        c_new = wn / (den * s_tile)
    outs = []
    for hp in range(N_HEADS_A // 2):
        sl = slice(hp * LANES, (hp + 1) * LANES)
        h0 = 2 * hp
        out = jnp.where(lo, c_new[:, h0:h0 + 1], c_new[:, h0 + 1:h0 + 2]) * pair_out[hp]
        if l_prev is not None:
            cp2 = jnp.where(lo, c_prev[:, h0:h0 + 1], c_prev[:, h0 + 1:h0 + 2])
            out = out + cp2 * o_prev[:, sl].astype(jnp.float32)
        outs.append(out)
    return outs, l_new


def _band_valid(tq, j, nblk):
    tk = tq + 2 * BAND_RADIUS
    row = lax.broadcasted_iota(jnp.int32, (tq, tk), 0)
    col = lax.broadcasted_iota(jnp.int32, (tq, tk), 1)
    rel = col - row
    valid = (rel >= 0) & (rel <= 2 * BAND_RADIUS)
    valid = valid & ((col >= BAND_RADIUS) | (j > 0))
    return valid & ((col < tq + BAND_RADIUS) | (j < nblk - 1))


def _attn_d1_kernel(q_ref, kl_ref, km_ref, kr_ref, vl_ref, vm_ref, vr_ref, op_ref, lp_ref, o_ref, *, nblk):
    tq = q_ref.shape[0]
    valid = _band_valid(tq, pl.program_id(1), nblk)
    kcat = jnp.concatenate([kl_ref[...], km_ref[...], kr_ref[...]], axis=0)
    vcat = jnp.concatenate([vl_ref[...], vm_ref[...], vr_ref[...]], axis=0)
    outs, _ = _band_block(q_ref[...], kcat, vcat, valid, op_ref[...], lp_ref[...])
    for hp, out in enumerate(outs):
        o_ref[:, hp * LANES:(hp + 1) * LANES] = out.astype(o_ref.dtype)


def _attn_d16_kernel(q_ref, kl_ref, km_ref, kr_ref, vl_ref, vm_ref, vr_ref, o_ref, l_ref, *, nblk):
    def rows(ref):
        return jnp.concatenate([ref[a] for a in range(ref.shape[0])], axis=0)

    nq = q_ref.shape[0]
    tq = nq * LJ
    valid = _band_valid(tq, pl.program_id(2), nblk)
    kcat = jnp.concatenate([rows(kl_ref), rows(km_ref), rows(kr_ref)], axis=0)
    vcat = jnp.concatenate([rows(vl_ref), rows(vm_ref), rows(vr_ref)], axis=0)
    outs, l_new = _band_block(rows(q_ref), kcat, vcat, valid, None, None)
    for a in range(nq):
        for hp, out in enumerate(outs):
            o_ref[a, :, hp * LANES:(hp + 1) * LANES] = out[a * LJ:(a + 1) * LJ].astype(o_ref.dtype)
        l_ref[a] = l_new[a * LJ:(a + 1) * LJ]


def _d4_rel():
    half = LJ // 2
    sub_tile = LT // 4
    mq = np.array([4 * j + q for q in range(4) for j in range(LJ)])
    prev = np.array([4 * (half + j) + q - sub_tile for q in range(4) for j in range(half)])
    nxt = np.array([sub_tile + 4 * j + q for q in range(4) for j in range(half)])
    mk = np.concatenate([prev, mq, nxt])
    return (mk[None, :] - mq[:, None]).astype(np.int32)


def _attn_d4_kernel(q_ref, kp_ref, km_ref, kn_ref, vp_ref, vm_ref, vn_ref, op_ref, lp_ref, rel_ref, pt_ref,
                    o_ref, l_ref, ores_ref, lres_ref, *, ntiles):
    t = pl.program_id(1)
    half = LJ // 2
    nsub = LT // 4
    rel = rel_ref[...]
    col = lax.broadcasted_iota(jnp.int32, rel.shape, 1)
    valid = (rel >= -BAND_RADIUS) & (rel <= BAND_RADIUS)
    valid = valid & ((col >= 4 * half) | (t > 0)) & ((col < 4 * half + nsub) | (t < ntiles - 1))
    for r4 in range(4):
        res = [4 * q + r4 for q in range(4)]

        def main(ref):
            return jnp.concatenate([ref[r * LJ:(r + 1) * LJ, :] for r in res], axis=0)

        def window(p_ref, m_ref, n_ref):
            return jnp.concatenate([p_ref[r] for r in res] + [main(m_ref)] + [n_ref[r] for r in res], axis=0)

        outs, l_new = _band_block(main(q_ref), window(kp_ref, km_ref, kn_ref), window(vp_ref, vm_ref, vn_ref),
                                  valid, main(op_ref), main(lp_ref))
        for qi, r in enumerate(res):
            for hp, out in enumerate(outs):
                ores_ref[r * LJ:(r + 1) * LJ, hp * LANES:(hp + 1) * LANES] = (
                    out[qi * LJ:(qi + 1) * LJ].astype(ores_ref.dtype))
            lres_ref[r * LJ:(r + 1) * LJ, :] = l_new[qi * LJ:(qi + 1) * LJ]

    pt = pt_ref[...]
    o_ref[...] = jnp.dot(pt, ores_ref[...], preferred_element_type=jnp.float32).astype(o_ref.dtype)
    lres = lres_ref[...]
    l_a = lres.astype(jnp.bfloat16)
    rem = lres - l_a.astype(jnp.float32)
    l_b = rem.astype(jnp.bfloat16)
    l_c = (rem - l_b.astype(jnp.float32)).astype(jnp.bfloat16)
    l_ref[...] = (jnp.dot(pt, l_a, preferred_element_type=jnp.float32)
                  + jnp.dot(pt, l_b, preferred_element_type=jnp.float32)
                  + jnp.dot(pt, l_c, preferred_element_type=jnp.float32))


def _attn_d16(qres, b, s):
    nt = s // LT
    nq = TQ_A // LJ
    nh = BAND_RADIUS // LJ
    nblk = nt // nq
    qv = qres.reshape(b, nt, LR, LJ, 3 * HALF_D)

    def main_spec(cb, width=HALF_D):
        return pl.BlockSpec((None, nq, None, LJ, width), lambda bi, r, j: (bi, j, r, 0, cb))

    def left_spec(cb):
        return pl.BlockSpec((None, nh, None, LJ, HALF_D),
                            lambda bi, r, j: (bi, jnp.maximum(j * (nq // nh) - 1, 0), r, 0, cb))

    def right_spec(cb):
        return pl.BlockSpec((None, nh, None, LJ, HALF_D),
                            lambda bi, r, j: (bi, jnp.minimum((j + 1) * (nq // nh), nt // nh - 1), r, 0, cb))

    return pl.pallas_call(
        functools.partial(_attn_d16_kernel, nblk=nblk),
        grid=(b, LR, nblk),
        in_specs=[main_spec(0), left_spec(1), main_spec(1), right_spec(1), left_spec(2), main_spec(2), right_spec(2)],
        out_specs=[main_spec(0), main_spec(0, LANES)],
        out_shape=[jax.ShapeDtypeStruct((b, nt, LR, LJ, HALF_D), jnp.bfloat16),
                   jax.ShapeDtypeStruct((b, nt, LR, LJ, LANES), jnp.float32)],
        compiler_params=_cparams(("parallel", "parallel", "parallel"), VMEM_LIMIT),
        name="attn_dilated_16",
    )(*([qv] * 7))


def _attn_d4(qres, o_prev, l_prev, rel, perm_t, b, s):
    nt = s // LT
    half = LJ // 2
    qv = qres.reshape(b, nt, LT, 3 * HALF_D)
    qh = qres.reshape(b, nt, LR, 2, half, 3 * HALF_D)

    def tile_spec(cb, width=HALF_D):
        return pl.BlockSpec((None, None, LT, width), lambda bi, t: (bi, t, 0, cb))

    def prev_spec(cb):
        return pl.BlockSpec((None, None, LR, None, half, HALF_D),
                            lambda bi, t: (bi, jnp.maximum(t - 1, 0), 0, 1, 0, cb))

    def next_spec(cb):
        return pl.BlockSpec((None, None, LR, None, half, HALF_D),
                            lambda bi, t: (bi, jnp.minimum(t + 1, nt - 1), 0, 0, 0, cb))

    return pl.pallas_call(
        functools.partial(_attn_d4_kernel, ntiles=nt),
        grid=(b, nt),
        in_specs=[tile_spec(0), prev_spec(1), tile_spec(1), next_spec(1), prev_spec(2), tile_spec(2), next_spec(2),
                  tile_spec(0), tile_spec(0, LANES),
                  pl.BlockSpec(rel.shape, lambda bi, t: (0, 0)), pl.BlockSpec((LT, LT), lambda bi, t: (0, 0))],
        out_specs=[pl.BlockSpec((None, LT, HALF_D), lambda bi, t: (bi, t, 0)),
                   pl.BlockSpec((None, LT, LANES), lambda bi, t: (bi, t, 0))],
        out_shape=[jax.ShapeDtypeStruct((b, s, HALF_D), jnp.bfloat16),
                   jax.ShapeDtypeStruct((b, s, LANES), jnp.float32)],
        scratch_shapes=[pltpu.VMEM((LT, HALF_D), jnp.bfloat16), pltpu.VMEM((LT, LANES), jnp.float32)],
        compiler_params=_cparams(("parallel", "parallel"), VMEM_LIMIT),
        name="attn_dilated_4",
    )(qv, qh, qv, qh, qh, qv, qh, o_prev.reshape(b, nt, LT, HALF_D), l_prev.reshape(b, nt, LT, LANES), rel, perm_t)


def _attn_d1(qkv, o_prev, l_prev):
    b, s, _ = qkv.shape
    tq = TQ_A
    nblk = s // tq
    per = tq // BAND_RADIUS
    nhalo = s // BAND_RADIUS

    def main_spec(cb, width=HALF_D):
        return pl.BlockSpec((None, tq, width), lambda bi, j: (bi, j, cb))

    def left_spec(cb):
        return pl.BlockSpec((None, BAND_RADIUS, HALF_D), lambda bi, j: (bi, jnp.maximum(j * per - 1, 0), cb))

    def right_spec(cb):
        return pl.BlockSpec((None, BAND_RADIUS, HALF_D),
                            lambda bi, j: (bi, jnp.minimum((j + 1) * per, nhalo - 1), cb))

    return pl.pallas_call(
        functools.partial(_attn_d1_kernel, nblk=nblk),
        grid=(b, nblk),
        in_specs=[main_spec(0), left_spec(2), main_spec(2), right_spec(2), left_spec(4), main_spec(4), right_spec(4),
                  main_spec(0), main_spec(0, LANES)],
        out_specs=main_spec(0),
        out_shape=jax.ShapeDtypeStruct((b, s, HALF_D), jnp.bfloat16),
        compiler_params=_cparams(("parallel", "parallel"), VMEM_LIMIT),
        name="attn_dilated_1",
    )(qkv, qkv, qkv, qkv, qkv, qkv, qkv, o_prev, l_prev)


def _attn_b_kernel(q_ref, kp_ref, kc_ref, kn_ref, vp_ref, vc_ref, vn_ref, t2_ref, o_ref, kw_ref, vw_ref, *, rows):
    ib = pl.program_id(1)
    blk = TB_ROWS * GRID_W
    kw_ref[0:blk] = kp_ref[...]
    kw_ref[blk:2 * blk] = kc_ref[...]
    kw_ref[2 * blk:3 * blk] = kn_ref[...]
    vw_ref[0:blk] = vp_ref[...]
    vw_ref[blk:2 * blk] = vc_ref[...]
    vw_ref[2 * blk:3 * blk] = vn_ref[...]
    lane = lax.broadcasted_iota(jnp.int32, (GRID_W, LANES), 1)
    lo = lane < HEAD_DIM
    nkeys = NA_ROWS * GRID_W

    for hp in range(N_HEADS_B // 2):
        sl = slice(hp * LANES, (hp + 1) * LANES)

        def body(t, carry, sl=sl, hp=hp):
            i = ib * TB_ROWS + t
            rs = jnp.clip(i - NA_ROWS // 2, 0, rows - NA_ROWS)
            off = pl.multiple_of((rs - ib * TB_ROWS + TB_ROWS) * GRID_W, GRID_W)
            dlt = rs - i + (NA_ROWS - 1)
            qrow = pl.multiple_of(t * GRID_W, GRID_W)
            q2 = q_ref[pl.ds(qrow, GRID_W), sl]
            k2 = kw_ref[pl.ds(off, nkeys), sl]
            v2 = vw_ref[pl.ds(off, nkeys), sl]
            zero = jnp.zeros_like(q2)
            qs = jnp.concatenate([jnp.where(lo, q2, zero), jnp.where(lo, zero, q2)], axis=0)
            s = lax.dot_general(qs, k2, _NT, preferred_element_type=jnp.float32)
            bias = jnp.concatenate(
                [jnp.concatenate([t2_ref[hp * 2 + hh, dlt + 2 * kp] for kp in range(NA_ROWS // 2)], axis=1)
                 for hh in range(2)], axis=0)
            s = s + bias
            m = jnp.max(s, axis=1, keepdims=True)
            p = jnp.exp(s - m)
            ssum = jnp.sum(p, axis=1, keepdims=True)
            o2 = jnp.dot(p.astype(jnp.bfloat16), v2, preferred_element_type=jnp.float32) / ssum
            o_ref[pl.ds(qrow, GRID_W), sl] = jnp.where(lo, o2[0:GRID_W], o2[GRID_W:2 * GRID_W]).astype(o_ref.dtype)
            return carry

        lax.fori_loop(0, TB_ROWS, body, 0, unroll=True)


def _bias_table(rpb):
    c = np.arange(GRID_W)
    cstart = np.clip(c - NA_COLS // 2, 0, GRID_W - NA_COLS)
    colmask = (c[None, :] >= cstart[:, None]) & (c[None, :] < cstart[:, None] + NA_COLS)
    coff = np.clip(c[None, :] - c[:, None], 1 - NA_COLS, NA_COLS - 1) + (NA_COLS - 1)
    bc = rpb[:, :, coff].astype(jnp.float32)
    bc = jnp.where(colmask[None, None], bc, NEG_INF)
    return jnp.concatenate([bc[:, :-1], bc[:, 1:]], axis=-1)


def _attn_b(qkv, t2):
    b, s, _ = qkv.shape
    rows = s // GRID_W
    blk = TB_ROWS * GRID_W
    nb = s // blk

    def spec(cb, shift):
        return pl.BlockSpec((None, blk, HALF_D),
                            lambda bi, ib: (bi, jnp.clip(ib + shift, 0, nb - 1), cb))

    return pl.pallas_call(
        functools.partial(_attn_b_kernel, rows=rows),
        grid=(b, nb),
        in_specs=[spec(1, 0), spec(3, -1), spec(3, 0), spec(3, 1), spec(5, -1), spec(5, 0), spec(5, 1),
                  pl.BlockSpec(t2.shape, lambda bi, ib: (0, 0, 0, 0))],
        out_specs=pl.BlockSpec((None, blk, HALF_D), lambda bi, ib: (bi, ib, 0)),
        out_shape=jax.ShapeDtypeStruct((b, s, HALF_D), jnp.bfloat16),
        scratch_shapes=[pltpu.VMEM((3 * blk, HALF_D), jnp.bfloat16), pltpu.VMEM((3 * blk, HALF_D), jnp.bfloat16)],
        compiler_params=_cparams(("parallel", "parallel"), VMEM_LIMIT),
        name="attn_neighbourhood",
    )(qkv, qkv, qkv, qkv, qkv, qkv, qkv, t2)


def _outln_kernel(oa_ref, ob_ref, x_ref, w_ref, g_ref, b_ref, wr_ref, rb_ref, o_ref, meta_ref, cnt_ref, base_ref):
    h = jnp.dot(oa_ref[...], w_ref[0:HALF_D, :], preferred_element_type=jnp.float32)
    h = h + jnp.dot(ob_ref[...], w_ref[HALF_D:D_MODEL, :], preferred_element_type=jnp.float32)
    z = DEEPNORM_ALPHA * x_ref[...] + h
    y = _layer_norm(z, g_ref[...], b_ref[...])
    o_ref[:, 0:D_MODEL] = y
    _route_tile(y, wr_ref, rb_ref, base_ref, o_ref.at[:, D_MODEL:ROW_W], meta_ref, cnt_ref)


def _out_proj_ln(oa, ob, x2, t, w_bf, g, b, wr_t, rbias):
    r_in, r_out, r_shape, r_scratch = _route_io(t, TM_LN)
    return pl.pallas_call(
        _outln_kernel,
        grid=(t // TM_LN,),
        in_specs=[
            pl.BlockSpec((TM_LN, HALF_D), lambda i: (i, 0)),
            pl.BlockSpec((TM_LN, HALF_D), lambda i: (i, 0)),
            pl.BlockSpec((TM_LN, D_MODEL), lambda i: (i, 0)),
            pl.BlockSpec((D_MODEL, D_MODEL), lambda i: (0, 0)),
            pl.BlockSpec((1, D_MODEL), lambda i: (0, 0)),
            pl.BlockSpec((1, D_MODEL), lambda i: (0, 0)),
        ] + r_in,
        out_specs=r_out,
        out_shape=r_shape,
        scratch_shapes=[r_scratch],
        compiler_params=_cparams(("arbitrary",), VMEM_LIMIT),
        name="out_proj_ln",
    )(oa, ob, x2, w_bf, g, b, wr_t, rbias)


def _poolln_kernel(xm_ref, xp_ref, xn_ref, w_ref, sc_ref, g_ref, b_ref, wr_ref, rb_ref,
                   o_ref, meta_ref, cnt_ref, xe_ref, base_ref, *, seq):
    i = pl.program_id(0)
    tm = xm_ref.shape[0]
    tiles_per_seq = seq // tm
    it = i % tiles_per_seq
    x = xm_ref[...]
    xe_ref[0:POOL_HALO, :] = jnp.where(it > 0, xp_ref[...], 0.0)
    xe_ref[POOL_HALO:POOL_HALO + tm, :] = x
    xe_ref[POOL_HALO + tm:2 * POOL_HALO + tm, :] = jnp.where(it < tiles_per_seq - 1, xn_ref[...], 0.0)
    pos = it * tm + lax.broadcasted_iota(jnp.int32, (tm, 1), 0)
    ys = []
    for gi, win in enumerate(POOL_WINDOWS):
        half = win // 2
        cs = slice(gi * POOL_CH, (gi + 1) * POOL_CH)
        tot = xe_ref[POOL_HALO - half:POOL_HALO - half + tm, cs]
        for o in range(-half + 1, half):
            tot = tot + xe_ref[POOL_HALO + o:POOL_HALO + o + tm, cs]
        cnt = (jnp.minimum(pos + half, seq) - jnp.maximum(pos - half, 0)).astype(jnp.float32)
        u = tot / cnt - x[:, cs]
        ys.append(jnp.dot(u.astype(jnp.bfloat16), w_ref[gi], preferred_element_type=jnp.float32))
    y = jnp.concatenate(ys, axis=1) * sc_ref[...]
    z = DEEPNORM_ALPHA * x + y
    yn = _layer_norm(z, g_ref[...], b_ref[...])
    o_ref[:, 0:D_MODEL] = yn
    _route_tile(yn, wr_ref, rb_ref, base_ref, o_ref.at[:, D_MODEL:ROW_W], meta_ref, cnt_ref)


def _pool_ln(x2, t, w_bf, scale, g, b, wr_t, rbias, seq):
    hb = TM_POOL // POOL_HALO
    nhb = t // POOL_HALO
    r_in, r_out, r_shape, r_scratch = _route_io(t, TM_POOL)
    return pl.pallas_call(
        functools.partial(_poolln_kernel, seq=seq),
        grid=(t // TM_POOL,),
        in_specs=[
            pl.BlockSpec((TM_POOL, D_MODEL), lambda i: (i, 0)),
            pl.BlockSpec((POOL_HALO, D_MODEL), lambda i: (jnp.maximum(i * hb - 1, 0), 0)),
            pl.BlockSpec((POOL_HALO, D_MODEL), lambda i: (jnp.minimum((i + 1) * hb, nhb - 1), 0)),
            pl.BlockSpec(w_bf.shape, lambda i: (0, 0, 0)),
            pl.BlockSpec((1, D_MODEL), lambda i: (0, 0)),
            pl.BlockSpec((1, D_MODEL), lambda i: (0, 0)),
            pl.BlockSpec((1, D_MODEL), lambda i: (0, 0)),
        ] + r_in,
        out_specs=r_out,
        out_shape=r_shape,
        scratch_shapes=[pltpu.VMEM((TM_POOL + 2 * POOL_HALO, D_MODEL), jnp.float32), r_scratch],
        compiler_params=_cparams(("arbitrary",), VMEM_LIMIT),
        name="pool_ln",
    )(x2, x2, x2, w_bf, scale, g, b, wr_t, rbias)


def _route_tile(x, wr_ref, rb_ref, base_ref, gate_ref, meta_ref, cnt_ref):
    tr = x.shape[0]

    @pl.when(pl.program_id(0) == 0)
    def _():
        base_ref[...] = jnp.zeros_like(base_ref)

    xhi = x.astype(jnp.bfloat16)
    xlo = (x - xhi.astype(jnp.float32)).astype(jnp.bfloat16)
    l1 = lax.dot_general(wr_ref[...], xhi, _NT, preferred_element_type=jnp.float32)
    l2 = lax.dot_general(wr_ref[0:N_EXPERTS, :], xlo, _NT, preferred_element_type=jnp.float32)
    logits = l1[0:N_EXPERTS] + l1[N_EXPERTS:2 * N_EXPERTS] + l2
    aff = 1.0 / (1.0 + jnp.exp(-logits))
    sel = aff + rb_ref[...]

    best = jnp.full((1, tr), -jnp.inf, jnp.float32)
    cls = jnp.zeros((1, tr), jnp.int32)
    aa = jnp.zeros((1, tr), jnp.float32)
    ab = jnp.zeros((1, tr), jnp.float32)
    for c in range(N_CLASSES):
        ea, eb = CLASS_EA[c], CLASS_EB[c]
        ps = sel[ea:ea + 1] + sel[eb:eb + 1]
        better = ps > best
        best = jnp.where(better, ps, best)
        cls = jnp.where(better, c, cls)
        aa = jnp.where(better, aff[ea:ea + 1], aa)
        ab = jnp.where(better, aff[eb:eb + 1], ab)
    ga = aa / (aa + ab)
    gb = ab / (aa + ab)

    crow = lax.broadcasted_iota(jnp.int32, (CLASS_ROWS, tr), 0)
    onehot = (crow == cls).astype(jnp.float32)
    r_i = lax.broadcasted_iota(jnp.int32, (tr, tr), 0)
    c_i = lax.broadcasted_iota(jnp.int32, (tr, tr), 1)
    upper = (r_i < c_i).astype(jnp.bfloat16)
    excl = jnp.dot(onehot.astype(jnp.bfloat16), upper, preferred_element_type=jnp.float32)
    base = base_ref[...]
    rank = jnp.sum(onehot * (excl + base[:, 0:1]), axis=0, keepdims=True)
    base = base + jnp.sum(onehot, axis=1, keepdims=True)
    base_ref[...] = base
    cnt_ref[...] = base.astype(jnp.int32)

    mrow = lax.broadcasted_iota(jnp.int32, (8, tr), 0)
    meta_ref[...] = jnp.where(mrow == 0, cls, jnp.where(mrow == 1, rank.astype(jnp.int32), 0))
    grow = lax.broadcasted_iota(jnp.int32, (LANES, tr), 0)
    gmat = jnp.where(grow == 0, ga, jnp.where(grow == 1, gb, 0.0))
    gate_ref[...] = gmat.T


def _route_io(t, tm):
    in_specs = [pl.BlockSpec((2 * N_EXPERTS, D_MODEL), lambda i: (0, 0)),
                pl.BlockSpec((N_EXPERTS, 1), lambda i: (0, 0))]
    out_specs = [pl.BlockSpec((tm, ROW_W), lambda i: (i, 0)),
                 pl.BlockSpec((8, tm), lambda i: (0, i)),
                 pl.BlockSpec((CLASS_ROWS, LANES), lambda i: (0, 0))]
    out_shape = [jax.ShapeDtypeStruct((t, ROW_W), jnp.float32),
                 jax.ShapeDtypeStruct((8, t), jnp.int32),
                 jax.ShapeDtypeStruct((CLASS_ROWS, LANES), jnp.int32)]
    return in_specs, out_specs, out_shape, pltpu.VMEM((CLASS_ROWS, LANES), jnp.float32)


def _row_copy(src_hbm, dst_ref, src_row, dst_row, sem):
    return pltpu.make_async_copy(src_hbm.at[pl.ds(src_row, 1)], dst_ref.at[pl.ds(dst_row, 1)], sem)


def _moe_ffn_kernel(dest_ref, ta_ref, tb_ref, nv_ref, xw_hbm, w1a_ref, w3a_ref, w2a_ref, w1b_ref, w3b_ref, w2b_ref,
                    g_ref, b_ref, o_hbm, xbuf, obuf, h_ref, src_ref, gsem, ssem, *, t_rows, n_tiles):
    del ta_ref, tb_ref
    i = pl.program_id(0)
    nv = nv_ref[0]
    tm = TM_FFN
    chunk = 256
    n_chunks = 2 * D_MODEL // chunk
    rows_per_chunk = tm // n_chunks

    def gather_copy(tile, r, slot):
        tok = jnp.maximum(src_ref[tile * tm + r], 0)
        return _row_copy(xw_hbm, xbuf.at[slot], tok, r, gsem.at[slot])

    def scatter_copy(tile, r, slot, discard):
        s = src_ref[tile * tm + r]
        tok = jnp.where((s < 0) | discard, t_rows + slot * tm + r, s)
        return _row_copy(obuf.at[slot], o_hbm, r, tok, ssem.at[slot])

    def wait_gather(slot):
        pltpu.make_async_copy(xw_hbm.at[pl.ds(0, tm)], xbuf.at[slot], gsem.at[slot]).wait()

    def wait_scatter(slot):
        pltpu.make_async_copy(obuf.at[slot], o_hbm.at[pl.ds(0, tm)], ssem.at[slot]).wait()

    @pl.when(i == 0)
    def _():
        def clear(p, carry):
            src_ref[p] = -1
            return carry
        lax.fori_loop(0, n_tiles * tm, clear, 0, unroll=16)

        def invert(tok, carry):
            src_ref[dest_ref[tok]] = tok
            return carry
        lax.fori_loop(0, t_rows, invert, 0, unroll=8)

        obuf[...] = jnp.zeros_like(obuf)
        spare = [pltpu.make_async_copy(obuf.at[k % 2], o_hbm.at[pl.ds(t_rows + k * tm, tm)], ssem.at[0])
                 for k in range(ROW_PAD // tm)]
        for cp in spare:
            cp.start()
        for cp in spare:
            cp.wait()

        def issue(r, carry):
            gather_copy(0, r, 0).start()
            return carry
        lax.fori_loop(0, tm, issue, 0, unroll=8)

    @pl.when(i < nv)
    def _():
        slot = i % 2
        other = 1 - slot
        nxt = jnp.minimum(i + 1, n_tiles - 1)
        prv = jnp.maximum(i - 1, 0)
        first = i == 0
        wait_gather(slot)
        x = xbuf[slot, :, 0:D_MODEL].astype(jnp.bfloat16)
        ci = 0
        for e, (w1_ref, w3_ref) in enumerate(((w1a_ref, w3a_ref), (w1b_ref, w3b_ref))):
            g = xbuf[slot, :, D_MODEL + e:D_MODEL + e + 1]
            for c in range(D_MODEL // chunk):
                cs = slice(c * chunk, (c + 1) * chunk)
                a = jnp.dot(x, w1_ref[:, cs], preferred_element_type=jnp.float32)
                bb = jnp.dot(x, w3_ref[:, cs], preferred_element_type=jnp.float32)
                hcol = (a / (1.0 + jnp.exp(-a))) * bb * g
                h_ref[:, e * D_MODEL + c * chunk:e * D_MODEL + (c + 1) * chunk] = hcol.astype(jnp.bfloat16)
                for r in range(ci * rows_per_chunk, (ci + 1) * rows_per_chunk):
                    gather_copy(nxt, r, other).start()
                    scatter_copy(prv, r, other, first).start()
                ci += 1
        y = jnp.dot(h_ref[:, 0:D_MODEL], w2a_ref[...], preferred_element_type=jnp.float32)
        y = y + jnp.dot(h_ref[:, D_MODEL:2 * D_MODEL], w2b_ref[...], preferred_element_type=jnp.float32)
        z = DEEPNORM_ALPHA * xbuf[slot, :, 0:D_MODEL] + y
        out = _layer_norm(z, g_ref[...], b_ref[...])

        @pl.when(i >= 1)
        def _():
            wait_scatter(slot)

        obuf[slot] = out

        @pl.when(i == nv - 1)
        def _():
            def issue(r, carry):
                scatter_copy(i, r, slot, False).start()
                return carry
            lax.fori_loop(0, tm, issue, 0, unroll=8)
            wait_scatter(other)
            wait_scatter(slot)
            wait_gather(other)


def _moe_ffn(dest, ta, tb, nvalid, xw, w1, w3, w2, layer, g, b):
    t_rows = dest.shape[0]
    nt = ta.shape[0]

    def wspec(which):
        if which == 0:
            return pl.BlockSpec((None, None, D_MODEL, D_MODEL), lambda i, s, ta, tb, nv: (layer, ta[i], 0, 0))
        return pl.BlockSpec((None, None, D_MODEL, D_MODEL), lambda i, s, ta, tb, nv: (layer, tb[i], 0, 0))

    vec = pl.BlockSpec((1, D_MODEL), lambda i, s, ta, tb, nv: (0, 0))
    return pl.pallas_call(
        functools.partial(_moe_ffn_kernel, t_rows=t_rows, n_tiles=nt),
        grid_spec=pltpu.PrefetchScalarGridSpec(
            num_scalar_prefetch=4,
            grid=(nt,),
            in_specs=[pl.BlockSpec(memory_space=pl.ANY),
                      wspec(0), wspec(0), wspec(0), wspec(1), wspec(1), wspec(1), vec, vec],
            out_specs=pl.BlockSpec(memory_space=pl.ANY),
            scratch_shapes=[pltpu.VMEM((2, TM_FFN, ROW_W), jnp.float32),
                            pltpu.VMEM((2, TM_FFN, D_MODEL), jnp.float32),
                            pltpu.VMEM((TM_FFN, 2 * D_MODEL), jnp.bfloat16),
                            pltpu.SMEM((nt * TM_FFN,), jnp.int32),
                            pltpu.SemaphoreType.DMA((2,)), pltpu.SemaphoreType.DMA((2,))],
        ),
        out_shape=jax.ShapeDtypeStruct((t_rows + ROW_PAD, D_MODEL), jnp.float32),
        compiler_params=_cparams(("arbitrary",), VMEM_LIMIT),
        name="moe_ffn",
    )(dest, ta, tb, nvalid, xw, w1, w3, w2, w1, w3, w2, g, b)


def _moe_ln(xw, meta, cnt, w1, w3, w2, layer, g, b):
    t = xw.shape[0]
    cls, rank = meta[0], meta[1]
    counts = cnt[:N_CLASSES, 0]
    tiles_c = (counts + TM_FFN - 1) // TM_FFN
    tile_end = jnp.cumsum(tiles_c)
    row_start = (tile_end - tiles_c) * TM_FFN
    dest = row_start[cls] + rank
    nt = t // TM_FFN + N_CLASSES
    nvalid = tile_end[-1]
    tidx = jnp.arange(nt, dtype=jnp.int32)
    tcls = jnp.sum(tile_end[None, :] <= jnp.minimum(tidx, nvalid - 1)[:, None], axis=1).astype(jnp.int32)
    ta = jnp.asarray(CLASS_EA, jnp.int32)[tcls]
    tb = jnp.asarray(CLASS_EB, jnp.int32)[tcls]
    return _moe_ffn(dest, ta, tb, nvalid.reshape(1).astype(jnp.int32), xw, w1, w3, w2, layer, g, b)


def _rope_tables(seq):
    pos = jnp.arange(seq, dtype=jnp.float32)
    inv_freq = ROPE_THETA ** (-jnp.arange(0, HEAD_DIM, 2, dtype=jnp.float32) / HEAD_DIM)
    ang = pos[:, None] * inv_freq[None, :]
    reps = LANES // (HEAD_DIM // 2)
    return jnp.tile(jnp.cos(ang), (1, reps)), jnp.tile(jnp.sin(ang), (1, reps))


def kernel(x, w_in, w_out, rpb, pool_w, pool_scale, router_w, router_bias, moe_w1, moe_w3, moe_w2, ln_g, ln_b):
    b, s, d = x.shape
    t = b * s
    bf = jnp.bfloat16
    cos_t, sin_t = _rope_tables(s)
    perm_np = _residue_perm()
    perm, perm_t = jnp.asarray(perm_np, bf), jnp.asarray(perm_np.T, bf)
    d4_rel = jnp.asarray(_d4_rel())
    rw_hi = router_w.astype(bf)
    rw_lo = (router_w - rw_hi.astype(jnp.float32)).astype(bf)
    wr_t = jnp.concatenate([rw_hi.T, rw_lo.T], axis=0)
    rbias = router_bias.astype(jnp.float32).reshape(N_EXPERTS, 1)
    w_in_bf, w_out_bf, pool_w_bf = w_in.astype(bf), w_out.astype(bf), pool_w.astype(bf)
    w1_bf, w3_bf, w2_bf = moe_w1.astype(bf), moe_w3.astype(bf), moe_w2.astype(bf)

    x2 = x.reshape(t, d)
    for layer in range(DEPTH):
        i = layer // 2
        g0, b0 = ln_g[layer, 0].reshape(1, d), ln_b[layer, 0].reshape(1, d)
        g1, b1 = ln_g[layer, 1].reshape(1, d), ln_b[layer, 1].reshape(1, d)
        if layer % 2 == 0:
            qkv, qres = _qkv_proj(x2, t, w_in_bf[i], cos_t, sin_t, perm, s)
            qkv = qkv.reshape(b, s, 3 * d)
            oa, la = _attn_d16(qres, b, s)
            oa, la = _attn_d4(qres, oa, la, d4_rel, perm_t, b, s)
            oa = _attn_d1(qkv, oa, la)
            ob = _attn_b(qkv, _bias_table(rpb[i]))
            xw, meta, cnt = _out_proj_ln(oa.reshape(t, HALF_D), ob.reshape(t, HALF_D), x2, t, w_out_bf[i], g0, b0,
                                         wr_t, rbias)
        else:
            xw, meta, cnt = _pool_ln(x2, t, pool_w_bf[i], pool_scale[i].reshape(1, d), g0, b0, wr_t, rbias, s)
        x2 = _moe_ln(xw, meta, cnt, w1_bf, w3_bf, w2_bf, layer, g1, b1)
    return x2[:t].reshape(b, s, d)
```

```python
import functools

import numpy as np
import jax
import jax.numpy as jnp
from jax import lax
from jax.experimental import pallas as pl
from jax.experimental.pallas import tpu as pltpu

D_MODEL = 1024
HEAD_DIM = 64
N_HEADS_A = 8
N_HEADS_B = 8
HALF_D = N_HEADS_A * HEAD_DIM
ROPE_THETA = 10000.0
DILATED_PATTERNS = ((128, 1), (512, 4), (2048, 16))
BAND_RADIUS = 64
GRID_W = 64
NA_ROWS = 8
NA_COLS = 16
POOL_WINDOWS = (2, 4, 8, 16)
POOL_CH = D_MODEL // len(POOL_WINDOWS)
POOL_HALO = 8
N_EXPERTS = 16
EXPERTS_PER_GROUP = 4
DEPTH = 4
DEEPNORM_ALPHA = (2.0 * DEPTH) ** 0.25
LN_EPS = 1e-5
NEG_INF = -1e30

LANES = 128
GATE_COLS = LANES
ROW_W = D_MODEL + GATE_COLS

_PAIR_CHAIN = ((0, 1), (0, 2), (1, 2), (1, 3), (0, 3), (2, 3))
CLASS_EA = tuple(g * EXPERTS_PER_GROUP + a for g in range(4) for a, _ in _PAIR_CHAIN)
CLASS_EB = tuple(g * EXPERTS_PER_GROUP + b for g in range(4) for _, b in _PAIR_CHAIN)
N_CLASSES = len(CLASS_EA)
CLASS_ROWS = 32

LT = 512
LR = 16
LJ = LT // LR

TM_QKV = 1024
TN_QKV = 512
TQ_A = 256
TB_ROWS = 8
TM_LN = 256
TM_POOL = 256
TM_FFN = 256
ROW_PAD = 1024
VMEM_LIMIT = 48 * 1024 * 1024

_NT = (((1,), (1,)), ((), ()))


def _cparams(sem, vmem=None):
    return pltpu.CompilerParams(dimension_semantics=sem, vmem_limit_bytes=vmem)


def _layer_norm(z, g, b):
    mu = jnp.mean(z, axis=-1, keepdims=True)
    zc = z - mu
    var = jnp.mean(zc * zc, axis=-1, keepdims=True)
    return zc * lax.rsqrt(var + LN_EPS) * g + b


def _qkv_kernel(x_ref, w_ref, cos_ref, sin_ref, p_ref, o_ref, r_ref, xb_ref):
    n = pl.program_id(1)

    @pl.when(n == 0)
    def _():
        xb_ref[...] = x_ref[...].astype(jnp.bfloat16)

    acc = jnp.dot(xb_ref[...], w_ref[...], preferred_element_type=jnp.float32)
    scale = jnp.where(n < 2, HEAD_DIM ** -0.5, 1.0).astype(jnp.float32)

    @pl.when((n == 0) | (n == 2))
    def _():
        cos = cos_ref[...]
        sin = sin_ref[...]
        lane = lax.broadcasted_iota(jnp.int32, cos.shape, 1)
        first = (lane % HEAD_DIM) < (HEAD_DIM // 2)
        for c in range(TN_QKV // LANES):
            a = acc[:, c * LANES:(c + 1) * LANES]
            nxt = pltpu.roll(a, LANES - HEAD_DIM // 2, 1)
            prv = pltpu.roll(a, HEAD_DIM // 2, 1)
            rot = a * cos + jnp.where(first, -nxt, prv) * sin
            o_ref[:, c * LANES:(c + 1) * LANES] = (rot * scale).astype(o_ref.dtype)

    @pl.when((n != 0) & (n != 2))
    def _():
        o_ref[...] = (acc * scale).astype(o_ref.dtype)

    @pl.when(n % 2 == 0)
    def _():
        for hlf in range(TM_QKV // LT):
            rs = slice(hlf * LT, (hlf + 1) * LT)
            r_ref[rs, :] = jnp.dot(p_ref[...], o_ref[rs, :], preferred_element_type=jnp.float32).astype(r_ref.dtype)


def _residue_perm():
    p = np.zeros((LT, LT), np.float32)
    j, r = np.meshgrid(np.arange(LJ), np.arange(LR))
    p[(r * LJ + j).ravel(), (LR * j + r).ravel()] = 1.0
    return p


def _qkv_proj(x2, t, w_bf, cos_t, sin_t, perm, seq):
    tiles_per_seq = seq // TM_QKV
    return pl.pallas_call(
        _qkv_kernel,
        grid=(t // TM_QKV, 3 * D_MODEL // TN_QKV),
        in_specs=[
            pl.BlockSpec((TM_QKV, D_MODEL), lambda i, n: (i, 0)),
            pl.BlockSpec((D_MODEL, TN_QKV), lambda i, n: (0, n)),
            pl.BlockSpec((TM_QKV, LANES), lambda i, n: (i % tiles_per_seq, 0)),
            pl.BlockSpec((TM_QKV, LANES), lambda i, n: (i % tiles_per_seq, 0)),
            pl.BlockSpec((LT, LT), lambda i, n: (0, 0)),
        ],
        out_specs=[pl.BlockSpec((TM_QKV, TN_QKV), lambda i, n: (i, n)),
                   pl.BlockSpec((TM_QKV, HALF_D), lambda i, n: (i, n // 2))],
        out_shape=[jax.ShapeDtypeStruct((t, 3 * D_MODEL), jnp.bfloat16),
                   jax.ShapeDtypeStruct((t, 3 * HALF_D), jnp.bfloat16)],
        scratch_shapes=[pltpu.VMEM((TM_QKV, D_MODEL), jnp.bfloat16)],
        compiler_params=_cparams(("parallel", "arbitrary"), VMEM_LIMIT),
        name="qkv_proj",
    )(x2, w_bf, cos_t, sin_t, perm)


def _band_block(q, kcat, vcat, valid, o_prev, l_prev):
    tq = q.shape[0]
    lane = lax.broadcasted_iota(jnp.int32, (tq, LANES), 1)
    lo = lane < HEAD_DIM
    valid2 = jnp.concatenate([valid, valid], axis=0)
    m_tile = jnp.zeros((tq, LANES), jnp.float32)
    s_tile = jnp.ones((tq, LANES), jnp.float32)
    pair_out = []
    for hp in range(N_HEADS_A // 2):
        sl = slice(hp * LANES, (hp + 1) * LANES)
        q2, k2, v2 = q[:, sl], kcat[:, sl], vcat[:, sl]
        zero = jnp.zeros_like(q2)
        qs = jnp.concatenate([jnp.where(lo, q2, zero), jnp.where(lo, zero, q2)], axis=0)
        s = lax.dot_general(qs, k2, _NT, preferred_element_type=jnp.float32)
        s = jnp.where(valid2, s, NEG_INF)
        m = jnp.max(s, axis=1, keepdims=True)
        p = jnp.exp(s - m)
        ssum = jnp.sum(p, axis=1, keepdims=True)
        o2 = jnp.dot(p.astype(jnp.bfloat16), v2, preferred_element_type=jnp.float32)
        pair_out.append(jnp.where(lo, o2[0:tq], o2[tq:2 * tq]))
        m_tile = jnp.where(lane == 2 * hp, m[0:tq], jnp.where(lane == 2 * hp + 1, m[tq:2 * tq], m_tile))
        s_tile = jnp.where(lane == 2 * hp, ssum[0:tq], jnp.where(lane == 2 * hp + 1, ssum[tq:2 * tq], s_tile))

    lse = m_tile + jnp.log(s_tile)
    if l_prev is None:
        c_new = 1.0 / s_tile
        l_new = lse
    else:
        m2 = jnp.maximum(l_prev, lse)
        wp = jnp.exp(l_prev - m2)
        wn = jnp.exp(lse - m2)
        den = wp + wn
        l_new = m2 + jnp.log(den)
        c_prev = wp / den
        c_new = wn / (den * s_tile)
    outs = []
    for hp in range(N_HEADS_A // 2):
        sl = slice(hp * LANES, (hp + 1) * LANES)
        h0 = 2 * hp
        out = jnp.where(lo, c_new[:, h0:h0 + 1], c_new[:, h0 + 1:h0 + 2]) * pair_out[hp]
        if l_prev is not None:
            cp2 = jnp.where(lo, c_prev[:, h0:h0 + 1], c_prev[:, h0 + 1:h0 + 2])
            out = out + cp2 * o_prev[:, sl].astype(jnp.float32)
        outs.append(out)
    return outs, l_new


def _band_valid(tq, j, nblk):
    tk = tq + 2 * BAND_RADIUS
    row = lax.broadcasted_iota(jnp.int32, (tq, tk), 0)
    col = lax.broadcasted_iota(jnp.int32, (tq, tk), 1)
    rel = col - row
    valid = (rel >= 0) & (rel <= 2 * BAND_RADIUS)
    valid = valid & ((col >= BAND_RADIUS) | (j > 0))
    return valid & ((col < tq + BAND_RADIUS) | (j < nblk - 1))


def _attn_d1_kernel(q_ref, kl_ref, km_ref, kr_ref, vl_ref, vm_ref, vr_ref, op_ref, lp_ref, o_ref, *, nblk):
    tq = q_ref.shape[0]
    valid = _band_valid(tq, pl.program_id(1), nblk)
    kcat = jnp.concatenate([kl_ref[...], km_ref[...], kr_ref[...]], axis=0)
    vcat = jnp.concatenate([vl_ref[...], vm_ref[...], vr_ref[...]], axis=0)
    outs, _ = _band_block(q_ref[...], kcat, vcat, valid, op_ref[...], lp_ref[...])
    for hp, out in enumerate(outs):
        o_ref[:, hp * LANES:(hp + 1) * LANES] = out.astype(o_ref.dtype)


def _attn_d16_kernel(q_ref, kl_ref, km_ref, kr_ref, vl_ref, vm_ref, vr_ref, o_ref, l_ref, *, nblk):
    def rows(ref):
        return jnp.concatenate([ref[a] for a in range(ref.shape[0])], axis=0)

    nq = q_ref.shape[0]
    tq = nq * LJ
    valid = _band_valid(tq, pl.program_id(2), nblk)
    kcat = jnp.concatenate([rows(kl_ref), rows(km_ref), rows(kr_ref)], axis=0)
    vcat = jnp.concatenate([rows(vl_ref), rows(vm_ref), rows(vr_ref)], axis=0)
    outs, l_new = _band_block(rows(q_ref), kcat, vcat, valid, None, None)
    for a in range(nq):
        for hp, out in enumerate(outs):
            o_ref[a, :, hp * LANES:(hp + 1) * LANES] = out[a * LJ:(a + 1) * LJ].astype(o_ref.dtype)
        l_ref[a] = l_new[a * LJ:(a + 1) * LJ]


def _d4_rel():
    half = LJ // 2
    sub_tile = LT // 4
    mq = np.array([4 * j + q for q in range(4) for j in range(LJ)])
    prev = np.array([4 * (half + j) + q - sub_tile for q in range(4) for j in range(half)])
    nxt = np.array([sub_tile + 4 * j + q for q in range(4) for j in range(half)])
    mk = np.concatenate([prev, mq, nxt])
    return (mk[None, :] - mq[:, None]).astype(np.int32)


def _attn_d4_kernel(q_ref, kp_ref, km_ref, kn_ref, vp_ref, vm_ref, vn_ref, op_ref, lp_ref, rel_ref, pt_ref,
                    o_ref, l_ref, ores_ref, lres_ref, *, ntiles):
    t = pl.program_id(1)
    half = LJ // 2
    nsub = LT // 4
    rel = rel_ref[...]
    col = lax.broadcasted_iota(jnp.int32, rel.shape, 1)
    valid = (rel >= -BAND_RADIUS) & (rel <= BAND_RADIUS)
    valid = valid & ((col >= 4 * half) | (t > 0)) & ((col < 4 * half + nsub) | (t < ntiles - 1))
    for r4 in range(4):
        res = [4 * q + r4 for q in range(4)]

        def main(ref):
            return jnp.concatenate([ref[r * LJ:(r + 1) * LJ, :] for r in res], axis=0)

        def window(p_ref, m_ref, n_ref):
            return jnp.concatenate([p_ref[r] for r in res] + [main(m_ref)] + [n_ref[r] for r in res], axis=0)

        outs, l_new = _band_block(main(q_ref), window(kp_ref, km_ref, kn_ref), window(vp_ref, vm_ref, vn_ref),
                                  valid, main(op_ref), main(lp_ref))
        for qi, r in enumerate(res):
            for hp, out in enumerate(outs):
                ores_ref[r * LJ:(r + 1) * LJ, hp * LANES:(hp + 1) * LANES] = (
                    out[qi * LJ:(qi + 1) * LJ].astype(ores_ref.dtype))
            lres_ref[r * LJ:(r + 1) * LJ, :] = l_new[qi * LJ:(qi + 1) * LJ]

    pt = pt_ref[...]
    o_ref[...] = jnp.dot(pt, ores_ref[...], preferred_element_type=jnp.float32).astype(o_ref.dtype)
    lres = lres_ref[...]
    l_a = lres.astype(jnp.bfloat16)
    rem = lres - l_a.astype(jnp.float32)
    l_b = rem.astype(jnp.bfloat16)
    l_c = (rem - l_b.astype(jnp.float32)).astype(jnp.bfloat16)
    l_ref[...] = (jnp.dot(pt, l_a, preferred_element_type=jnp.float32)
                  + jnp.dot(pt, l_b, preferred_element_type=jnp.float32)
                  + jnp.dot(pt, l_c, preferred_element_type=jnp.float32))


def _attn_d16(qres, b, s):
    nt = s // LT
    nq = TQ_A // LJ
    nh = BAND_RADIUS // LJ
    nblk = nt // nq
    qv = qres.reshape(b, nt, LR, LJ, 3 * HALF_D)

    def main_spec(cb, width=HALF_D):
        return pl.BlockSpec((None, nq, None, LJ, width), lambda bi, r, j: (bi, j, r, 0, cb))

    def left_spec(cb):
        return pl.BlockSpec((None, nh, None, LJ, HALF_D),
                            lambda bi, r, j: (bi, jnp.maximum(j * (nq // nh) - 1, 0), r, 0, cb))

    def right_spec(cb):
        return pl.BlockSpec((None, nh, None, LJ, HALF_D),
                            lambda bi, r, j: (bi, jnp.minimum((j + 1) * (nq // nh), nt // nh - 1), r, 0, cb))

    return pl.pallas_call(
        functools.partial(_attn_d16_kernel, nblk=nblk),
        grid=(b, LR, nblk),
        in_specs=[main_spec(0), left_spec(1), main_spec(1), right_spec(1), left_spec(2), main_spec(2), right_spec(2)],
        out_specs=[main_spec(0), main_spec(0, LANES)],
        out_shape=[jax.ShapeDtypeStruct((b, nt, LR, LJ, HALF_D), jnp.bfloat16),
                   jax.ShapeDtypeStruct((b, nt, LR, LJ, LANES), jnp.float32)],
        compiler_params=_cparams(("parallel", "parallel", "parallel"), VMEM_LIMIT),
        name="attn_dilated_16",
    )(*([qv] * 7))


def _attn_d4(qres, o_prev, l_prev, rel, perm_t, b, s):
    nt = s // LT
    half = LJ // 2
    qv = qres.reshape(b, nt, LT, 3 * HALF_D)
    qh = qres.reshape(b, nt, LR, 2, half, 3 * HALF_D)

    def tile_spec(cb, width=HALF_D):
        return pl.BlockSpec((None, None, LT, width), lambda bi, t: (bi, t, 0, cb))

    def prev_spec(cb):
        return pl.BlockSpec((None, None, LR, None, half, HALF_D),
                            lambda bi, t: (bi, jnp.maximum(t - 1, 0), 0, 1, 0, cb))

    def next_spec(cb):
        return pl.BlockSpec((None, None, LR, None, half, HALF_D),
                            lambda bi, t: (bi, jnp.minimum(t + 1, nt - 1), 0, 0, 0, cb))

    return pl.pallas_call(
        functools.partial(_attn_d4_kernel, ntiles=nt),
        grid=(b, nt),
        in_specs=[tile_spec(0), prev_spec(1), tile_spec(1), next_spec(1), prev_spec(2), tile_spec(2), next_spec(2),
                  tile_spec(0), tile_spec(0, LANES),
                  pl.BlockSpec(rel.shape, lambda bi, t: (0, 0)), pl.BlockSpec((LT, LT), lambda bi, t: (0, 0))],
        out_specs=[pl.BlockSpec((None, LT, HALF_D), lambda bi, t: (bi, t, 0)),
                   pl.BlockSpec((None, LT, LANES), lambda bi, t: (bi, t, 0))],
        out_shape=[jax.ShapeDtypeStruct((b, s, HALF_D), jnp.bfloat16),
                   jax.ShapeDtypeStruct((b, s, LANES), jnp.float32)],
        scratch_shapes=[pltpu.VMEM((LT, HALF_D), jnp.bfloat16), pltpu.VMEM((LT, LANES), jnp.float32)],
        compiler_params=_cparams(("parallel", "parallel"), VMEM_LIMIT),
        name="attn_dilated_4",
    )(qv, qh, qv, qh, qh, qv, qh, o_prev.reshape(b, nt, LT, HALF_D), l_prev.reshape(b, nt, LT, LANES), rel, perm_t)


def _attn_d1(qkv, o_prev, l_prev):
    b, s, _ = qkv.shape
    tq = TQ_A
    nblk = s // tq
    per = tq // BAND_RADIUS
    nhalo = s // BAND_RADIUS

    def main_spec(cb, width=HALF_D):
        return pl.BlockSpec((None, tq, width), lambda bi, j: (bi, j, cb))

    def left_spec(cb):
        return pl.BlockSpec((None, BAND_RADIUS, HALF_D), lambda bi, j: (bi, jnp.maximum(j * per - 1, 0), cb))

    def right_spec(cb):
        return pl.BlockSpec((None, BAND_RADIUS, HALF_D),
                            lambda bi, j: (bi, jnp.minimum((j + 1) * per, nhalo - 1), cb))

    return pl.pallas_call(
        functools.partial(_attn_d1_kernel, nblk=nblk),
        grid=(b, nblk),
        in_specs=[main_spec(0), left_spec(2), main_spec(2), right_spec(2), left_spec(4), main_spec(4), right_spec(4),
                  main_spec(0), main_spec(0, LANES)],
        out_specs=main_spec(0),
        out_shape=jax.ShapeDtypeStruct((b, s, HALF_D), jnp.bfloat16),
        compiler_params=_cparams(("parallel", "parallel"), VMEM_LIMIT),
        name="attn_dilated_1",
    )(qkv, qkv, qkv, qkv, qkv, qkv, qkv, o_prev, l_prev)


def _attn_b_kernel(q_ref, kp_ref, kc_ref, kn_ref, vp_ref, vc_ref, vn_ref, t2_ref, o_ref, kw_ref, vw_ref, *, rows):
    ib = pl.program_id(1)
    blk = TB_ROWS * GRID_W
    kw_ref[0:blk] = kp_ref[...]
    kw_ref[blk:2 * blk] = kc_ref[...]
    kw_ref[2 * blk:3 * blk] = kn_ref[...]
    vw_ref[0:blk] = vp_ref[...]
    vw_ref[blk:2 * blk] = vc_ref[...]
    vw_ref[2 * blk:3 * blk] = vn_ref[...]
    lane = lax.broadcasted_iota(jnp.int32, (GRID_W, LANES), 1)
    lo = lane < HEAD_DIM
    nkeys = NA_ROWS * GRID_W

    for hp in range(N_HEADS_B // 2):
        sl = slice(hp * LANES, (hp + 1) * LANES)

        def body(t, carry, sl=sl, hp=hp):
            i = ib * TB_ROWS + t
            rs = jnp.clip(i - NA_ROWS // 2, 0, rows - NA_ROWS)
            off = pl.multiple_of((rs - ib * TB_ROWS + TB_ROWS) * GRID_W, GRID_W)
            dlt = rs - i + (NA_ROWS - 1)
            qrow = pl.multiple_of(t * GRID_W, GRID_W)
            q2 = q_ref[pl.ds(qrow, GRID_W), sl]
            k2 = kw_ref[pl.ds(off, nkeys), sl]
            v2 = vw_ref[pl.ds(off, nkeys), sl]
            zero = jnp.zeros_like(q2)
            qs = jnp.concatenate([jnp.where(lo, q2, zero), jnp.where(lo, zero, q2)], axis=0)
            s = lax.dot_general(qs, k2, _NT, preferred_element_type=jnp.float32)
            bias = jnp.concatenate(
                [jnp.concatenate([t2_ref[hp * 2 + hh, dlt + 2 * kp] for kp in range(NA_ROWS // 2)], axis=1)
                 for hh in range(2)], axis=0)
            s = s + bias
            m = jnp.max(s, axis=1, keepdims=True)
            p = jnp.exp(s - m)
            ssum = jnp.sum(p, axis=1, keepdims=True)
            o2 = jnp.dot(p.astype(jnp.bfloat16), v2, preferred_element_type=jnp.float32) / ssum
            o_ref[pl.ds(qrow, GRID_W), sl] = jnp.where(lo, o2[0:GRID_W], o2[GRID_W:2 * GRID_W]).astype(o_ref.dtype)
            return carry

        lax.fori_loop(0, TB_ROWS, body, 0, unroll=True)


def _bias_table(rpb):
    c = np.arange(GRID_W)
    cstart = np.clip(c - NA_COLS // 2, 0, GRID_W - NA_COLS)
    colmask = (c[None, :] >= cstart[:, None]) & (c[None, :] < cstart[:, None] + NA_COLS)
    coff = np.clip(c[None, :] - c[:, None], 1 - NA_COLS, NA_COLS - 1) + (NA_COLS - 1)
    bc = rpb[:, :, coff].astype(jnp.float32)
    bc = jnp.where(colmask[None, None], bc, NEG_INF)
    return jnp.concatenate([bc[:, :-1], bc[:, 1:]], axis=-1)


def _attn_b(qkv, t2):
    b, s, _ = qkv.shape
    rows = s // GRID_W
    blk = TB_ROWS * GRID_W
    nb = s // blk

    def spec(cb, shift):
        return pl.BlockSpec((None, blk, HALF_D),
                            lambda bi, ib: (bi, jnp.clip(ib + shift, 0, nb - 1), cb))

    return pl.pallas_call(
        functools.partial(_attn_b_kernel, rows=rows),
        grid=(b, nb),
        in_specs=[spec(1, 0), spec(3, -1), spec(3, 0), spec(3, 1), spec(5, -1), spec(5, 0), spec(5, 1),
                  pl.BlockSpec(t2.shape, lambda bi, ib: (0, 0, 0, 0))],
        out_specs=pl.BlockSpec((None, blk, HALF_D), lambda bi, ib: (bi, ib, 0)),
        out_shape=jax.ShapeDtypeStruct((b, s, HALF_D), jnp.bfloat16),
        scratch_shapes=[pltpu.VMEM((3 * blk, HALF_D), jnp.bfloat16), pltpu.VMEM((3 * blk, HALF_D), jnp.bfloat16)],
        compiler_params=_cparams(("parallel", "parallel"), VMEM_LIMIT),
        name="attn_neighbourhood",
    )(qkv, qkv, qkv, qkv, qkv, qkv, qkv, t2)


def _outln_kernel(oa_ref, ob_ref, x_ref, w_ref, g_ref, b_ref, wr_ref, rb_ref, o_ref, meta_ref, cnt_ref, base_ref):
    h = jnp.dot(oa_ref[...], w_ref[0:HALF_D, :], preferred_element_type=jnp.float32)
    h = h + jnp.dot(ob_ref[...], w_ref[HALF_D:D_MODEL, :], preferred_element_type=jnp.float32)
    z = DEEPNORM_ALPHA * x_ref[...] + h
    y = _layer_norm(z, g_ref[...], b_ref[...])
    o_ref[:, 0:D_MODEL] = y
    _route_tile(y, wr_ref, rb_ref, base_ref, o_ref.at[:, D_MODEL:ROW_W], meta_ref, cnt_ref)


def _out_proj_ln(oa, ob, x2, t, w_bf, g, b, wr_t, rbias):
    r_in, r_out, r_shape, r_scratch = _route_io(t, TM_LN)
    return pl.pallas_call(
        _outln_kernel,
        grid=(t // TM_LN,),
        in_specs=[
            pl.BlockSpec((TM_LN, HALF_D), lambda i: (i, 0)),
            pl.BlockSpec((TM_LN, HALF_D), lambda i: (i, 0)),
            pl.BlockSpec((TM_LN, D_MODEL), lambda i: (i, 0)),
            pl.BlockSpec((D_MODEL, D_MODEL), lambda i: (0, 0)),
            pl.BlockSpec((1, D_MODEL), lambda i: (0, 0)),
            pl.BlockSpec((1, D_MODEL), lambda i: (0, 0)),
        ] + r_in,
        out_specs=r_out,
        out_shape=r_shape,
        scratch_shapes=[r_scratch],
        compiler_params=_cparams(("arbitrary",), VMEM_LIMIT),
        name="out_proj_ln",
    )(oa, ob, x2, w_bf, g, b, wr_t, rbias)


def _poolln_kernel(xm_ref, xp_ref, xn_ref, w_ref, sc_ref, g_ref, b_ref, wr_ref, rb_ref,
                   o_ref, meta_ref, cnt_ref, xe_ref, base_ref, *, seq):
    i = pl.program_id(0)
    tm = xm_ref.shape[0]
    tiles_per_seq = seq // tm
    it = i % tiles_per_seq
    x = xm_ref[...]
    xe_ref[0:POOL_HALO, :] = jnp.where(it > 0, xp_ref[...], 0.0)
    xe_ref[POOL_HALO:POOL_HALO + tm, :] = x
    xe_ref[POOL_HALO + tm:2 * POOL_HALO + tm, :] = jnp.where(it < tiles_per_seq - 1, xn_ref[...], 0.0)
    pos = it * tm + lax.broadcasted_iota(jnp.int32, (tm, 1), 0)
    ys = []
    for gi, win in enumerate(POOL_WINDOWS):
        half = win // 2
        cs = slice(gi * POOL_CH, (gi + 1) * POOL_CH)
        tot = xe_ref[POOL_HALO - half:POOL_HALO - half + tm, cs]
        for o in range(-half + 1, half):
            tot = tot + xe_ref[POOL_HALO + o:POOL_HALO + o + tm, cs]
        cnt = (jnp.minimum(pos + half, seq) - jnp.maximum(pos - half, 0)).astype(jnp.float32)
        u = tot / cnt - x[:, cs]
        ys.append(jnp.dot(u.astype(jnp.bfloat16), w_ref[gi], preferred_element_type=jnp.float32))
    y = jnp.concatenate(ys, axis=1) * sc_ref[...]
    z = DEEPNORM_ALPHA * x + y
    yn = _layer_norm(z, g_ref[...], b_ref[...])
    o_ref[:, 0:D_MODEL] = yn
    _route_tile(yn, wr_ref, rb_ref, base_ref, o_ref.at[:, D_MODEL:ROW_W], meta_ref, cnt_ref)


def _pool_ln(x2, t, w_bf, scale, g, b, wr_t, rbias, seq):
    hb = TM_POOL // POOL_HALO
    nhb = t // POOL_HALO
    r_in, r_out, r_shape, r_scratch = _route_io(t, TM_POOL)
    return pl.pallas_call(
        functools.partial(_poolln_kernel, seq=seq),
        grid=(t // TM_POOL,),
        in_specs=[
            pl.BlockSpec((TM_POOL, D_MODEL), lambda i: (i, 0)),
            pl.BlockSpec((POOL_HALO, D_MODEL), lambda i: (jnp.maximum(i * hb - 1, 0), 0)),
            pl.BlockSpec((POOL_HALO, D_MODEL), lambda i: (jnp.minimum((i + 1) * hb, nhb - 1), 0)),
            pl.BlockSpec(w_bf.shape, lambda i: (0, 0, 0)),
            pl.BlockSpec((1, D_MODEL), lambda i: (0, 0)),
            pl.BlockSpec((1, D_MODEL), lambda i: (0, 0)),
            pl.BlockSpec((1, D_MODEL), lambda i: (0, 0)),
        ] + r_in,
        out_specs=r_out,
        out_shape=r_shape,
        scratch_shapes=[pltpu.VMEM((TM_POOL + 2 * POOL_HALO, D_MODEL), jnp.float32), r_scratch],
        compiler_params=_cparams(("arbitrary",), VMEM_LIMIT),
        name="pool_ln",
    )(x2, x2, x2, w_bf, scale, g, b, wr_t, rbias)


def _route_tile(x, wr_ref, rb_ref, base_ref, gate_ref, meta_ref, cnt_ref):
    tr = x.shape[0]

    @pl.when(pl.program_id(0) == 0)
    def _():
        base_ref[...] = jnp.zeros_like(base_ref)

    xhi = x.astype(jnp.bfloat16)
    xlo = (x - xhi.astype(jnp.float32)).astype(jnp.bfloat16)
    l1 = lax.dot_general(wr_ref[...], xhi, _NT, preferred_element_type=jnp.float32)
    l2 = lax.dot_general(wr_ref[0:N_EXPERTS, :], xlo, _NT, preferred_element_type=jnp.float32)
    logits = l1[0:N_EXPERTS] + l1[N_EXPERTS:2 * N_EXPERTS] + l2
    aff = 1.0 / (1.0 + jnp.exp(-logits))
    sel = aff + rb_ref[...]

    best = jnp.full((1, tr), -jnp.inf, jnp.float32)
    cls = jnp.zeros((1, tr), jnp.int32)
    aa = jnp.zeros((1, tr), jnp.float32)
    ab = jnp.zeros((1, tr), jnp.float32)
    for c in range(N_CLASSES):
        ea, eb = CLASS_EA[c], CLASS_EB[c]
        ps = sel[ea:ea + 1] + sel[eb:eb + 1]
        better = ps > best
        best = jnp.where(better, ps, best)
        cls = jnp.where(better, c, cls)
        aa = jnp.where(better, aff[ea:ea + 1], aa)
        ab = jnp.where(better, aff[eb:eb + 1], ab)
    ga = aa / (aa + ab)
    gb = ab / (aa + ab)

    crow = lax.broadcasted_iota(jnp.int32, (CLASS_ROWS, tr), 0)
    onehot = (crow == cls).astype(jnp.float32)
    r_i = lax.broadcasted_iota(jnp.int32, (tr, tr), 0)
    c_i = lax.broadcasted_iota(jnp.int32, (tr, tr), 1)
    upper = (r_i < c_i).astype(jnp.bfloat16)
    excl = jnp.dot(onehot.astype(jnp.bfloat16), upper, preferred_element_type=jnp.float32)
    base = base_ref[...]
    rank = jnp.sum(onehot * (excl + base[:, 0:1]), axis=0, keepdims=True)
    base = base + jnp.sum(onehot, axis=1, keepdims=True)
    base_ref[...] = base
    cnt_ref[...] = base.astype(jnp.int32)

    mrow = lax.broadcasted_iota(jnp.int32, (8, tr), 0)
    meta_ref[...] = jnp.where(mrow == 0, cls, jnp.where(mrow == 1, rank.astype(jnp.int32), 0))
    grow = lax.broadcasted_iota(jnp.int32, (LANES, tr), 0)
    gmat = jnp.where(grow == 0, ga, jnp.where(grow == 1, gb, 0.0))
    gate_ref[...] = gmat.T


def _route_io(t, tm):
    in_specs = [pl.BlockSpec((2 * N_EXPERTS, D_MODEL), lambda i: (0, 0)),
                pl.BlockSpec((N_EXPERTS, 1), lambda i: (0, 0))]
    out_specs = [pl.BlockSpec((tm, ROW_W), lambda i: (i, 0)),
                 pl.BlockSpec((8, tm), lambda i: (0, i)),
                 pl.BlockSpec((CLASS_ROWS, LANES), lambda i: (0, 0))]
    out_shape = [jax.ShapeDtypeStruct((t, ROW_W), jnp.float32),
                 jax.ShapeDtypeStruct((8, t), jnp.int32),
                 jax.ShapeDtypeStruct((CLASS_ROWS, LANES), jnp.int32)]
    return in_specs, out_specs, out_shape, pltpu.VMEM((CLASS_ROWS, LANES), jnp.float32)


def _row_copy(src_hbm, dst_ref, src_row, dst_row, sem):
    return pltpu.make_async_copy(src_hbm.at[pl.ds(src_row, 1)], dst_ref.at[pl.ds(dst_row, 1)], sem)


def _moe_ffn_kernel(dest_ref, ta_ref, tb_ref, nv_ref, xw_hbm, w1a_ref, w3a_ref, w2a_ref, w1b_ref, w3b_ref, w2b_ref,
                    g_ref, b_ref, o_hbm, xbuf, obuf, xb_ref, h_ref, src_ref, gsem, ssem, *, t_rows, n_tiles):
    del ta_ref, tb_ref
    i = pl.program_id(0)
    nv = nv_ref[0]
    tm = TM_FFN
    chunk = 256
    n_chunks = 2 * D_MODEL // chunk
    rows_per_chunk = tm // n_chunks
    gather_rows = -(-tm // (n_chunks - 2))

    def gather_copy(tile, r, slot):
        return _row_copy(xw_hbm, xbuf.at[slot], jnp.minimum(src_ref[tile * tm + r], t_rows - 1), r, gsem.at[slot])

    def scatter_copy(tile, r, slot):
        return _row_copy(obuf.at[slot], o_hbm, r, src_ref[tile * tm + r], ssem.at[slot])

    def wait_gather(slot):
        pltpu.make_async_copy(xw_hbm.at[pl.ds(0, tm)], xbuf.at[slot], gsem.at[slot]).wait()

    def wait_scatter(slot):
        pltpu.make_async_copy(obuf.at[slot], o_hbm.at[pl.ds(0, tm)], ssem.at[slot]).wait()

    @pl.when(i == 0)
    def _():
        def clear(p, carry):
            src_ref[p] = t_rows + (p & (2 * tm - 1))
            return carry
        lax.fori_loop(0, n_tiles * tm, clear, 0, unroll=16)

        def invert(tok, carry):
            src_ref[dest_ref[tok]] = tok
            return carry
        lax.fori_loop(0, t_rows, invert, 0, unroll=8)

        obuf[...] = jnp.zeros_like(obuf)
        spare = [pltpu.make_async_copy(obuf.at[k % 2], o_hbm.at[pl.ds(t_rows + k * tm, tm)], ssem.at[0])
                 for k in range(ROW_PAD // tm)]
        for cp in spare:
            cp.start()
        for cp in spare:
            cp.wait()

        def issue(r, carry):
            gather_copy(0, r, 0).start()
            return carry
        lax.fori_loop(0, tm, issue, 0, unroll=8)

    @pl.when(i < nv)
    def _():
        slot = i % 2
        other = 1 - slot
        nxt = i + 1
        prv = jnp.where(i == 0, n_tiles - 1, i - 1)
        wait_gather(slot)
        xb_ref[...] = xbuf[slot, :, 0:D_MODEL].astype(jnp.bfloat16)
        ci = 0
        for e, (w1_ref, w3_ref) in enumerate(((w1a_ref, w3a_ref), (w1b_ref, w3b_ref))):
            g = xbuf[slot, :, D_MODEL + e:D_MODEL + e + 1]
            for c in range(D_MODEL // chunk):
                cs = slice(c * chunk, (c + 1) * chunk)
                a = jnp.dot(xb_ref[...], w1_ref[:, cs], preferred_element_type=jnp.float32)
                for r in range(min(ci * gather_rows, tm), min((ci + 1) * gather_rows, tm)):
                    gather_copy(nxt, r, other).start()
                bb = jnp.dot(xb_ref[...], w3_ref[:, cs], preferred_element_type=jnp.float32)
                hcol = (a / (1.0 + jnp.exp(-a))) * bb * g
                h_ref[:, e * D_MODEL + c * chunk:e * D_MODEL + (c + 1) * chunk] = hcol.astype(jnp.bfloat16)
                for r in range(ci * rows_per_chunk, (ci + 1) * rows_per_chunk):
                    scatter_copy(prv, r, other).start()
                ci += 1
        y = jnp.dot(h_ref[:, 0:D_MODEL], w2a_ref[...], preferred_element_type=jnp.float32)
        y = y + jnp.dot(h_ref[:, D_MODEL:2 * D_MODEL], w2b_ref[...], preferred_element_type=jnp.float32)
        z = DEEPNORM_ALPHA * xbuf[slot, :, 0:D_MODEL] + y
        out = _layer_norm(z, g_ref[...], b_ref[...])

        @pl.when(i >= 1)
        def _():
            wait_scatter(slot)

        obuf[slot] = out

        @pl.when(i == nv - 1)
        def _():
            def issue(r, carry):
                scatter_copy(i, r, slot).start()
                return carry
            lax.fori_loop(0, tm, issue, 0, unroll=8)
            wait_scatter(other)
            wait_scatter(slot)
            wait_gather(other)


def _moe_ffn(dest, ta, tb, nvalid, xw, w1, w3, w2, layer, g, b):
    t_rows = dest.shape[0]
    nt = ta.shape[0]

    def wspec(which):
        if which == 0:
            return pl.BlockSpec((None, None, D_MODEL, D_MODEL), lambda i, s, ta, tb, nv: (layer, ta[i], 0, 0))
        return pl.BlockSpec((None, None, D_MODEL, D_MODEL), lambda i, s, ta, tb, nv: (layer, tb[i], 0, 0))

    vec = pl.BlockSpec((1, D_MODEL), lambda i, s, ta, tb, nv: (0, 0))
    return pl.pallas_call(
        functools.partial(_moe_ffn_kernel, t_rows=t_rows, n_tiles=nt),
        grid_spec=pltpu.PrefetchScalarGridSpec(
            num_scalar_prefetch=4,
            grid=(nt,),
            in_specs=[pl.BlockSpec(memory_space=pl.ANY),
                      wspec(0), wspec(0), wspec(0), wspec(1), wspec(1), wspec(1), vec, vec],
            out_specs=pl.BlockSpec(memory_space=pl.ANY),
            scratch_shapes=[pltpu.VMEM((2, TM_FFN, ROW_W), jnp.float32),
                            pltpu.VMEM((2, TM_FFN, D_MODEL), jnp.float32),
                            pltpu.VMEM((TM_FFN, D_MODEL), jnp.bfloat16),
                            pltpu.VMEM((TM_FFN, 2 * D_MODEL), jnp.bfloat16),
                            pltpu.SMEM((nt * TM_FFN,), jnp.int32),
                            pltpu.SemaphoreType.DMA((2,)), pltpu.SemaphoreType.DMA((2,))],
        ),
        out_shape=jax.ShapeDtypeStruct((t_rows + ROW_PAD, D_MODEL), jnp.float32),
        compiler_params=_cparams(("arbitrary",), VMEM_LIMIT),
        name="moe_ffn",
    )(dest, ta, tb, nvalid, xw, w1, w3, w2, w1, w3, w2, g, b)


def _moe_ln(xw, meta, cnt, w1, w3, w2, layer, g, b):
    t = xw.shape[0]
    cls, rank = meta[0], meta[1]
    counts = cnt[:N_CLASSES, 0]
    tiles_c = (counts + TM_FFN - 1) // TM_FFN
    tile_end = jnp.cumsum(tiles_c)
    row_start = (tile_end - tiles_c) * TM_FFN
    dest = row_start[cls] + rank
    nt = t // TM_FFN + N_CLASSES + 1
    nvalid = tile_end[-1]
    tidx = jnp.arange(nt, dtype=jnp.int32)
    tcls = jnp.sum(tile_end[None, :] <= jnp.minimum(tidx, nvalid - 1)[:, None], axis=1).astype(jnp.int32)
    ta = jnp.asarray(CLASS_EA, jnp.int32)[tcls]
    tb = jnp.asarray(CLASS_EB, jnp.int32)[tcls]
    return _moe_ffn(dest, ta, tb, nvalid.reshape(1).astype(jnp.int32), xw, w1, w3, w2, layer, g, b)


def _rope_tables(seq):
    pos = jnp.arange(seq, dtype=jnp.float32)
    inv_freq = ROPE_THETA ** (-jnp.arange(0, HEAD_DIM, 2, dtype=jnp.float32) / HEAD_DIM)
    ang = pos[:, None] * inv_freq[None, :]
    reps = LANES // (HEAD_DIM // 2)
    return jnp.tile(jnp.cos(ang), (1, reps)), jnp.tile(jnp.sin(ang), (1, reps))


def kernel(x, w_in, w_out, rpb, pool_w, pool_scale, router_w, router_bias, moe_w1, moe_w3, moe_w2, ln_g, ln_b):
    b, s, d = x.shape
    t = b * s
    bf = jnp.bfloat16
    cos_t, sin_t = _rope_tables(s)
    perm_np = _residue_perm()
    perm, perm_t = jnp.asarray(perm_np, bf), jnp.asarray(perm_np.T, bf)
    d4_rel = jnp.asarray(_d4_rel())
    rw_hi = router_w.astype(bf)
    rw_lo = (router_w - rw_hi.astype(jnp.float32)).astype(bf)
    wr_t = jnp.concatenate([rw_hi.T, rw_lo.T], axis=0)
    rbias = router_bias.astype(jnp.float32).reshape(N_EXPERTS, 1)
    w_in_bf, w_out_bf, pool_w_bf = w_in.astype(bf), w_out.astype(bf), pool_w.astype(bf)
    w1_bf, w3_bf, w2_bf = moe_w1.astype(bf), moe_w3.astype(bf), moe_w2.astype(bf)

    x2 = x.reshape(t, d)
    for layer in range(DEPTH):
        i = layer // 2
        g0, b0 = ln_g[layer, 0].reshape(1, d), ln_b[layer, 0].reshape(1, d)
        g1, b1 = ln_g[layer, 1].reshape(1, d), ln_b[layer, 1].reshape(1, d)
        if layer % 2 == 0:
            qkv, qres = _qkv_proj(x2, t, w_in_bf[i], cos_t, sin_t, perm, s)
            qkv = qkv.reshape(b, s, 3 * d)
            oa, la = _attn_d16(qres, b, s)
            oa, la = _attn_d4(qres, oa, la, d4_rel, perm_t, b, s)
            oa = _attn_d1(qkv, oa, la)
            ob = _attn_b(qkv, _bias_table(rpb[i]))
            xw, meta, cnt = _out_proj_ln(oa.reshape(t, HALF_D), ob.reshape(t, HALF_D), x2, t, w_out_bf[i], g0, b0,
                                         wr_t, rbias)
        else:
            xw, meta, cnt = _pool_ln(x2, t, pool_w_bf[i], pool_scale[i].reshape(1, d), g0, b0, wr_t, rbias, s)
        x2 = _moe_ln(xw, meta, cnt, w1_bf, w3_bf, w2_bf, layer, g1, b1)
    return x2[:t].reshape(b, s, d)
```

```python
import functools

import numpy as np
import jax
import jax.numpy as jnp
from jax import lax
from jax.experimental import pallas as pl
from jax.experimental.pallas import tpu as pltpu

D_MODEL = 1024
HEAD_DIM = 64
N_HEADS_A = 8
N_HEADS_B = 8
HALF_D = N_HEADS_A * HEAD_DIM
ROPE_THETA = 10000.0
DILATED_PATTERNS = ((128, 1), (512, 4), (2048, 16))
BAND_RADIUS = 64
GRID_W = 64
NA_ROWS = 8
NA_COLS = 16
POOL_WINDOWS = (2, 4, 8, 16)
POOL_CH = D_MODEL // len(POOL_WINDOWS)
POOL_HALO = 8
N_EXPERTS = 16
EXPERTS_PER_GROUP = 4
DEPTH = 4
DEEPNORM_ALPHA = (2.0 * DEPTH) ** 0.25
LN_EPS = 1e-5
NEG_INF = -1e30

LANES = 128
GATE_COLS = LANES
ROW_W = D_MODEL + GATE_COLS

_PAIR_CHAIN = ((0, 1), (0, 2), (1, 2), (1, 3), (0, 3), (2, 3))
CLASS_EA = tuple(g * EXPERTS_PER_GROUP + a for g in range(4) for a, _ in _PAIR_CHAIN)
CLASS_EB = tuple(g * EXPERTS_PER_GROUP + b for g in range(4) for _, b in _PAIR_CHAIN)
N_CLASSES = len(CLASS_EA)
CLASS_ROWS = 32

LT = 512
LR = 16
LJ = LT // LR

TM_QKV = 2048
TN_QKV = 512
TQ_A = 256
TB_ROWS = 8
TM_LN = 512
TM_POOL = 512
TM_FFN = 256
ROW_PAD = 1024
VMEM_LIMIT = 48 * 1024 * 1024

_NT = (((1,), (1,)), ((), ()))


def _cparams(sem, vmem=None):
    return pltpu.CompilerParams(dimension_semantics=sem, vmem_limit_bytes=vmem)


def _layer_norm(z, g, b):
    mu = jnp.mean(z, axis=-1, keepdims=True)
    zc = z - mu
    var = jnp.mean(zc * zc, axis=-1, keepdims=True)
    return zc * lax.rsqrt(var + LN_EPS) * g + b


def _qkv_kernel(x_ref, w_ref, cos_ref, sin_ref, p_ref, o_ref, r_ref, xb_ref):
    n = pl.program_id(1)

    @pl.when(n == 0)
    def _():
        xb_ref[...] = x_ref[...].astype(jnp.bfloat16)

    acc = jnp.dot(xb_ref[...], w_ref[...], preferred_element_type=jnp.float32)
    scale = jnp.where(n < 2, HEAD_DIM ** -0.5, 1.0).astype(jnp.float32)

    @pl.when((n == 0) | (n == 2))
    def _():
        cos = cos_ref[...]
        sin = sin_ref[...]
        lane = lax.broadcasted_iota(jnp.int32, cos.shape, 1)
        first = (lane % HEAD_DIM) < (HEAD_DIM // 2)
        for c in range(TN_QKV // LANES):
            a = acc[:, c * LANES:(c + 1) * LANES]
            nxt = pltpu.roll(a, LANES - HEAD_DIM // 2, 1)
            prv = pltpu.roll(a, HEAD_DIM // 2, 1)
            rot = a * cos + jnp.where(first, -nxt, prv) * sin
            o_ref[:, c * LANES:(c + 1) * LANES] = (rot * scale).astype(o_ref.dtype)

    @pl.when((n != 0) & (n != 2))
    def _():
        o_ref[...] = (acc * scale).astype(o_ref.dtype)

    @pl.when(n % 2 == 0)
    def _():
        for hlf in range(TM_QKV // LT):
            rs = slice(hlf * LT, (hlf + 1) * LT)
            r_ref[rs, :] = jnp.dot(p_ref[...], o_ref[rs, :], preferred_element_type=jnp.float32).astype(r_ref.dtype)


def _residue_perm():
    p = np.zeros((LT, LT), np.float32)
    j, r = np.meshgrid(np.arange(LJ), np.arange(LR))
    p[(r * LJ + j).ravel(), (LR * j + r).ravel()] = 1.0
    return p


def _qkv_proj(x2, t, w_bf, cos_t, sin_t, perm, seq):
    tiles_per_seq = seq // TM_QKV
    return pl.pallas_call(
        _qkv_kernel,
        grid=(t // TM_QKV, 3 * D_MODEL // TN_QKV),
        in_specs=[
            pl.BlockSpec((TM_QKV, D_MODEL), lambda i, n: (i, 0)),
            pl.BlockSpec((D_MODEL, TN_QKV), lambda i, n: (0, n)),
            pl.BlockSpec((TM_QKV, LANES), lambda i, n: (i % tiles_per_seq, 0)),
            pl.BlockSpec((TM_QKV, LANES), lambda i, n: (i % tiles_per_seq, 0)),
            pl.BlockSpec((LT, LT), lambda i, n: (0, 0)),
        ],
        out_specs=[pl.BlockSpec((TM_QKV, TN_QKV), lambda i, n: (i, n)),
                   pl.BlockSpec((TM_QKV, HALF_D), lambda i, n: (i, n // 2))],
        out_shape=[jax.ShapeDtypeStruct((t, 3 * D_MODEL), jnp.bfloat16),
                   jax.ShapeDtypeStruct((t, 3 * HALF_D), jnp.bfloat16)],
        scratch_shapes=[pltpu.VMEM((TM_QKV, D_MODEL), jnp.bfloat16)],
        compiler_params=_cparams(("parallel", "arbitrary"), VMEM_LIMIT),
        name="qkv_proj",
    )(x2, w_bf, cos_t, sin_t, perm)


def _band_block(q, kcat, vcat, valid, o_prev, l_prev):
    tq = q.shape[0]
    lane = lax.broadcasted_iota(jnp.int32, (tq, LANES), 1)
    lo = lane < HEAD_DIM
    valid2 = jnp.concatenate([valid, valid], axis=0)
    m_tile = jnp.zeros((tq, LANES), jnp.float32)
    s_tile = jnp.ones((tq, LANES), jnp.float32)
    pair_out = []
    for hp in range(N_HEADS_A // 2):
        sl = slice(hp * LANES, (hp + 1) * LANES)
        q2, k2, v2 = q[:, sl], kcat[:, sl], vcat[:, sl]
        zero = jnp.zeros_like(q2)
        qs = jnp.concatenate([jnp.where(lo, q2, zero), jnp.where(lo, zero, q2)], axis=0)
        s = lax.dot_general(qs, k2, _NT, preferred_element_type=jnp.float32)
        s = jnp.where(valid2, s, NEG_INF)
        m = jnp.max(s, axis=1, keepdims=True)
        p = jnp.exp(s - m)
        ssum = jnp.sum(p, axis=1, keepdims=True)
        o2 = jnp.dot(p.astype(jnp.bfloat16), v2, preferred_element_type=jnp.float32)
        pair_out.append(jnp.where(lo, o2[0:tq], o2[tq:2 * tq]))
        m_tile = jnp.where(lane == 2 * hp, m[0:tq], jnp.where(lane == 2 * hp + 1, m[tq:2 * tq], m_tile))
        s_tile = jnp.where(lane == 2 * hp, ssum[0:tq], jnp.where(lane == 2 * hp + 1, ssum[tq:2 * tq], s_tile))

    lse = m_tile + jnp.log(s_tile)
    if l_prev is None:
        c_new = 1.0 / s_tile
        l_new = lse
    else:
        m2 = jnp.maximum(l_prev, lse)
        wp = jnp.exp(l_prev - m2)
        wn = jnp.exp(lse - m2)
        den = wp + wn
        l_new = m2 + jnp.log(den)
        c_prev = wp / den
        c_new = wn / (den * s_tile)
    outs = []
    for hp in range(N_HEADS_A // 2):
        sl = slice(hp * LANES, (hp + 1) * LANES)
        h0 = 2 * hp
        out = jnp.where(lo, c_new[:, h0:h0 + 1], c_new[:, h0 + 1:h0 + 2]) * pair_out[hp]
        if l_prev is not None:
            cp2 = jnp.where(lo, c_prev[:, h0:h0 + 1], c_prev[:, h0 + 1:h0 + 2])
            out = out + cp2 * o_prev[:, sl].astype(jnp.float32)
        outs.append(out)
    return outs, l_new


def _band_valid(tq, j, nblk):
    tk = tq + 2 * BAND_RADIUS
    row = lax.broadcasted_iota(jnp.int32, (tq, tk), 0)
    col = lax.broadcasted_iota(jnp.int32, (tq, tk), 1)
    rel = col - row
    valid = (rel >= 0) & (rel <= 2 * BAND_RADIUS)
    valid = valid & ((col >= BAND_RADIUS) | (j > 0))
    return valid & ((col < tq + BAND_RADIUS) | (j < nblk - 1))


def _attn_d1_kernel(q_ref, kl_ref, km_ref, kr_ref, vl_ref, vm_ref, vr_ref, op_ref, lp_ref, o_ref, *, nblk):
    tq = q_ref.shape[0]
    valid = _band_valid(tq, pl.program_id(1), nblk)
    kcat = jnp.concatenate([kl_ref[...], km_ref[...], kr_ref[...]], axis=0)
    vcat = jnp.concatenate([vl_ref[...], vm_ref[...], vr_ref[...]], axis=0)
    outs, _ = _band_block(q_ref[...], kcat, vcat, valid, op_ref[...], lp_ref[...])
    for hp, out in enumerate(outs):
        o_ref[:, hp * LANES:(hp + 1) * LANES] = out.astype(o_ref.dtype)


def _attn_d16_kernel(q_ref, kl_ref, km_ref, kr_ref, vl_ref, vm_ref, vr_ref, o_ref, l_ref, *, nblk):
    def rows(ref):
        return jnp.concatenate([ref[a] for a in range(ref.shape[0])], axis=0)

    nq = q_ref.shape[0]
    tq = nq * LJ
    valid = _band_valid(tq, pl.program_id(2), nblk)
    kcat = jnp.concatenate([rows(kl_ref), rows(km_ref), rows(kr_ref)], axis=0)
    vcat = jnp.concatenate([rows(vl_ref), rows(vm_ref), rows(vr_ref)], axis=0)
    outs, l_new = _band_block(rows(q_ref), kcat, vcat, valid, None, None)
    for a in range(nq):
        for hp, out in enumerate(outs):
            o_ref[a, :, hp * LANES:(hp + 1) * LANES] = out[a * LJ:(a + 1) * LJ].astype(o_ref.dtype)
        l_ref[a] = l_new[a * LJ:(a + 1) * LJ]


def _d4_rel():
    half = LJ // 2
    sub_tile = LT // 4
    mq = np.array([4 * j + q for q in range(4) for j in range(LJ)])
    prev = np.array([4 * (half + j) + q - sub_tile for q in range(4) for j in range(half)])
    nxt = np.array([sub_tile + 4 * j + q for q in range(4) for j in range(half)])
    mk = np.concatenate([prev, mq, nxt])
    return (mk[None, :] - mq[:, None]).astype(np.int32)


def _attn_d4_kernel(q_ref, kp_ref, km_ref, kn_ref, vp_ref, vm_ref, vn_ref, op_ref, lp_ref, rel_ref, pt_ref,
                    o_ref, l_ref, ores_ref, lres_ref, *, ntiles):
    t = pl.program_id(1)
    half = LJ // 2
    nsub = LT // 4
    rel = rel_ref[...]
    col = lax.broadcasted_iota(jnp.int32, rel.shape, 1)
    valid = (rel >= -BAND_RADIUS) & (rel <= BAND_RADIUS)
    valid = valid & ((col >= 4 * half) | (t > 0)) & ((col < 4 * half + nsub) | (t < ntiles - 1))
    for r4 in range(4):
        res = [4 * q + r4 for q in range(4)]

        def main(ref):
            return jnp.concatenate([ref[r * LJ:(r + 1) * LJ, :] for r in res], axis=0)

        def window(p_ref, m_ref, n_ref):
            return jnp.concatenate([p_ref[r] for r in res] + [main(m_ref)] + [n_ref[r] for r in res], axis=0)

        outs, l_new = _band_block(main(q_ref), window(kp_ref, km_ref, kn_ref), window(vp_ref, vm_ref, vn_ref),
                                  valid, main(op_ref), main(lp_ref))
        for qi, r in enumerate(res):
            for hp, out in enumerate(outs):
                ores_ref[r * LJ:(r + 1) * LJ, hp * LANES:(hp + 1) * LANES] = (
                    out[qi * LJ:(qi + 1) * LJ].astype(ores_ref.dtype))
            lres_ref[r * LJ:(r + 1) * LJ, :] = l_new[qi * LJ:(qi + 1) * LJ]

    pt = pt_ref[...]
    o_ref[...] = jnp.dot(pt, ores_ref[...], preferred_element_type=jnp.float32).astype(o_ref.dtype)
    lres = lres_ref[...]
    l_a = lres.astype(jnp.bfloat16)
    rem = lres - l_a.astype(jnp.float32)
    l_b = rem.astype(jnp.bfloat16)
    l_c = (rem - l_b.astype(jnp.float32)).astype(jnp.bfloat16)
    l_ref[...] = (jnp.dot(pt, l_a, preferred_element_type=jnp.float32)
                  + jnp.dot(pt, l_b, preferred_element_type=jnp.float32)
                  + jnp.dot(pt, l_c, preferred_element_type=jnp.float32))


def _attn_d16(qres, b, s):
    nt = s // LT
    nq = TQ_A // LJ
    nh = BAND_RADIUS // LJ
    nblk = nt // nq
    qv = qres.reshape(b, nt, LR, LJ, 3 * HALF_D)

    def main_spec(cb, width=HALF_D):
        return pl.BlockSpec((None, nq, None, LJ, width), lambda bi, r, j: (bi, j, r, 0, cb))

    def left_spec(cb):
        return pl.BlockSpec((None, nh, None, LJ, HALF_D),
                            lambda bi, r, j: (bi, jnp.maximum(j * (nq // nh) - 1, 0), r, 0, cb))

    def right_spec(cb):
        return pl.BlockSpec((None, nh, None, LJ, HALF_D),
                            lambda bi, r, j: (bi, jnp.minimum((j + 1) * (nq // nh), nt // nh - 1), r, 0, cb))

    return pl.pallas_call(
        functools.partial(_attn_d16_kernel, nblk=nblk),
        grid=(b, LR, nblk),
        in_specs=[main_spec(0), left_spec(1), main_spec(1), right_spec(1), left_spec(2), main_spec(2), right_spec(2)],
        out_specs=[main_spec(0), main_spec(0, LANES)],
        out_shape=[jax.ShapeDtypeStruct((b, nt, LR, LJ, HALF_D), jnp.bfloat16),
                   jax.ShapeDtypeStruct((b, nt, LR, LJ, LANES), jnp.float32)],
        compiler_params=_cparams(("parallel", "parallel", "parallel"), VMEM_LIMIT),
        name="attn_dilated_16",
    )(*([qv] * 7))


def _attn_d4(qres, o_prev, l_prev, rel, perm_t, b, s):
    nt = s // LT
    half = LJ // 2
    qv = qres.reshape(b, nt, LT, 3 * HALF_D)
    qh = qres.reshape(b, nt, LR, 2, half, 3 * HALF_D)

    def tile_spec(cb, width=HALF_D):
        return pl.BlockSpec((None, None, LT, width), lambda bi, t: (bi, t, 0, cb))

    def prev_spec(cb):
        return pl.BlockSpec((None, None, LR, None, half, HALF_D),
                            lambda bi, t: (bi, jnp.maximum(t - 1, 0), 0, 1, 0, cb))

    def next_spec(cb):
        return pl.BlockSpec((None, None, LR, None, half, HALF_D),
                            lambda bi, t: (bi, jnp.minimum(t + 1, nt - 1), 0, 0, 0, cb))

    return pl.pallas_call(
        functools.partial(_attn_d4_kernel, ntiles=nt),
        grid=(b, nt),
        in_specs=[tile_spec(0), prev_spec(1), tile_spec(1), next_spec(1), prev_spec(2), tile_spec(2), next_spec(2),
                  tile_spec(0), tile_spec(0, LANES),
                  pl.BlockSpec(rel.shape, lambda bi, t: (0, 0)), pl.BlockSpec((LT, LT), lambda bi, t: (0, 0))],
        out_specs=[pl.BlockSpec((None, LT, HALF_D), lambda bi, t: (bi, t, 0)),
                   pl.BlockSpec((None, LT, LANES), lambda bi, t: (bi, t, 0))],
        out_shape=[jax.ShapeDtypeStruct((b, s, HALF_D), jnp.bfloat16),
                   jax.ShapeDtypeStruct((b, s, LANES), jnp.float32)],
        scratch_shapes=[pltpu.VMEM((LT, HALF_D), jnp.bfloat16), pltpu.VMEM((LT, LANES), jnp.float32)],
        compiler_params=_cparams(("parallel", "parallel"), VMEM_LIMIT),
        name="attn_dilated_4",
    )(qv, qh, qv, qh, qh, qv, qh, o_prev.reshape(b, nt, LT, HALF_D), l_prev.reshape(b, nt, LT, LANES), rel, perm_t)


def _attn_d1(qkv, o_prev, l_prev):
    b, s, _ = qkv.shape
    tq = TQ_A
    nblk = s // tq
    per = tq // BAND_RADIUS
    nhalo = s // BAND_RADIUS

    def main_spec(cb, width=HALF_D):
        return pl.BlockSpec((None, tq, width), lambda bi, j: (bi, j, cb))

    def left_spec(cb):
        return pl.BlockSpec((None, BAND_RADIUS, HALF_D), lambda bi, j: (bi, jnp.maximum(j * per - 1, 0), cb))

    def right_spec(cb):
        return pl.BlockSpec((None, BAND_RADIUS, HALF_D),
                            lambda bi, j: (bi, jnp.minimum((j + 1) * per, nhalo - 1), cb))

    return pl.pallas_call(
        functools.partial(_attn_d1_kernel, nblk=nblk),
        grid=(b, nblk),
        in_specs=[main_spec(0), left_spec(2), main_spec(2), right_spec(2), left_spec(4), main_spec(4), right_spec(4),
                  main_spec(0), main_spec(0, LANES)],
        out_specs=main_spec(0),
        out_shape=jax.ShapeDtypeStruct((b, s, HALF_D), jnp.bfloat16),
        compiler_params=_cparams(("parallel", "parallel"), VMEM_LIMIT),
        name="attn_dilated_1",
    )(qkv, qkv, qkv, qkv, qkv, qkv, qkv, o_prev, l_prev)


def _attn_b_kernel(q_ref, kp_ref, kc_ref, kn_ref, vp_ref, vc_ref, vn_ref, t2_ref, o_ref, kw_ref, vw_ref, *, rows):
    ib = pl.program_id(1)
    blk = TB_ROWS * GRID_W
    kw_ref[0:blk] = kp_ref[...]
    kw_ref[blk:2 * blk] = kc_ref[...]
    kw_ref[2 * blk:3 * blk] = kn_ref[...]
    vw_ref[0:blk] = vp_ref[...]
    vw_ref[blk:2 * blk] = vc_ref[...]
    vw_ref[2 * blk:3 * blk] = vn_ref[...]
    lane = lax.broadcasted_iota(jnp.int32, (GRID_W, LANES), 1)
    lo = lane < HEAD_DIM
    nkeys = NA_ROWS * GRID_W

    for hp in range(N_HEADS_B // 2):
        sl = slice(hp * LANES, (hp + 1) * LANES)

        def body(t, carry, sl=sl, hp=hp):
            i = ib * TB_ROWS + t
            rs = jnp.clip(i - NA_ROWS // 2, 0, rows - NA_ROWS)
            off = pl.multiple_of((rs - ib * TB_ROWS + TB_ROWS) * GRID_W, GRID_W)
            dlt = rs - i + (NA_ROWS - 1)
            qrow = pl.multiple_of(t * GRID_W, GRID_W)
            q2 = q_ref[pl.ds(qrow, GRID_W), sl]
            k2 = kw_ref[pl.ds(off, nkeys), sl]
            v2 = vw_ref[pl.ds(off, nkeys), sl]
            zero = jnp.zeros_like(q2)
            qs = jnp.concatenate([jnp.where(lo, q2, zero), jnp.where(lo, zero, q2)], axis=0)
            s = lax.dot_general(qs, k2, _NT, preferred_element_type=jnp.float32)
            bias = jnp.concatenate(
                [jnp.concatenate([t2_ref[hp * 2 + hh, dlt + 2 * kp] for kp in range(NA_ROWS // 2)], axis=1)
                 for hh in range(2)], axis=0)
            s = s + bias
            m = jnp.max(s, axis=1, keepdims=True)
            p = jnp.exp(s - m)
            ssum = jnp.sum(p, axis=1, keepdims=True)
            o2 = jnp.dot(p.astype(jnp.bfloat16), v2, preferred_element_type=jnp.float32) / ssum
            o_ref[pl.ds(qrow, GRID_W), sl] = jnp.where(lo, o2[0:GRID_W], o2[GRID_W:2 * GRID_W]).astype(o_ref.dtype)
            return carry

        lax.fori_loop(0, TB_ROWS, body, 0, unroll=True)


def _bias_table(rpb):
    c = np.arange(GRID_W)
    cstart = np.clip(c - NA_COLS // 2, 0, GRID_W - NA_COLS)
    colmask = (c[None, :] >= cstart[:, None]) & (c[None, :] < cstart[:, None] + NA_COLS)
    coff = np.clip(c[None, :] - c[:, None], 1 - NA_COLS, NA_COLS - 1) + (NA_COLS - 1)
    bc = rpb[:, :, coff].astype(jnp.float32)
    bc = jnp.where(colmask[None, None], bc, NEG_INF)
    return jnp.concatenate([bc[:, :-1], bc[:, 1:]], axis=-1)


def _attn_b(qkv, t2):
    b, s, _ = qkv.shape
    rows = s // GRID_W
    blk = TB_ROWS * GRID_W
    nb = s // blk

    def spec(cb, shift):
        return pl.BlockSpec((None, blk, HALF_D),
                            lambda bi, ib: (bi, jnp.clip(ib + shift, 0, nb - 1), cb))

    return pl.pallas_call(
        functools.partial(_attn_b_kernel, rows=rows),
        grid=(b, nb),
        in_specs=[spec(1, 0), spec(3, -1), spec(3, 0), spec(3, 1), spec(5, -1), spec(5, 0), spec(5, 1),
                  pl.BlockSpec(t2.shape, lambda bi, ib: (0, 0, 0, 0))],
        out_specs=pl.BlockSpec((None, blk, HALF_D), lambda bi, ib: (bi, ib, 0)),
        out_shape=jax.ShapeDtypeStruct((b, s, HALF_D), jnp.bfloat16),
        scratch_shapes=[pltpu.VMEM((3 * blk, HALF_D), jnp.bfloat16), pltpu.VMEM((3 * blk, HALF_D), jnp.bfloat16)],
        compiler_params=_cparams(("parallel", "parallel"), VMEM_LIMIT),
        name="attn_neighbourhood",
    )(qkv, qkv, qkv, qkv, qkv, qkv, qkv, t2)


def _outln_kernel(oa_ref, ob_ref, x_ref, w_ref, g_ref, b_ref, wr_ref, rb_ref, o_ref, meta_ref, cnt_ref, base_ref):
    h = jnp.dot(oa_ref[...], w_ref[0:HALF_D, :], preferred_element_type=jnp.float32)
    h = h + jnp.dot(ob_ref[...], w_ref[HALF_D:D_MODEL, :], preferred_element_type=jnp.float32)
    z = DEEPNORM_ALPHA * x_ref[...] + h
    y = _layer_norm(z, g_ref[...], b_ref[...])
    o_ref[:, 0:D_MODEL] = y
    _route_tile(y, wr_ref, rb_ref, base_ref, o_ref.at[:, D_MODEL:ROW_W], meta_ref, cnt_ref)


def _out_proj_ln(oa, ob, x2, t, w_bf, g, b, wr_t, rbias):
    r_in, r_out, r_shape, r_scratch = _route_io(t, TM_LN)
    return pl.pallas_call(
        _outln_kernel,
        grid=(t // TM_LN,),
        in_specs=[
            pl.BlockSpec((TM_LN, HALF_D), lambda i: (i, 0)),
            pl.BlockSpec((TM_LN, HALF_D), lambda i: (i, 0)),
            pl.BlockSpec((TM_LN, D_MODEL), lambda i: (i, 0)),
            pl.BlockSpec((D_MODEL, D_MODEL), lambda i: (0, 0)),
            pl.BlockSpec((1, D_MODEL), lambda i: (0, 0)),
            pl.BlockSpec((1, D_MODEL), lambda i: (0, 0)),
        ] + r_in,
        out_specs=r_out,
        out_shape=r_shape,
        scratch_shapes=[r_scratch],
        compiler_params=_cparams(("arbitrary",), VMEM_LIMIT),
        name="out_proj_ln",
    )(oa, ob, x2, w_bf, g, b, wr_t, rbias)


def _poolln_kernel(xm_ref, xp_ref, xn_ref, w_ref, sc_ref, g_ref, b_ref, wr_ref, rb_ref,
                   o_ref, meta_ref, cnt_ref, xe_ref, base_ref, *, seq):
    i = pl.program_id(0)
    tm = xm_ref.shape[0]
    tiles_per_seq = seq // tm
    it = i % tiles_per_seq
    x = xm_ref[...]
    xe_ref[0:POOL_HALO, :] = jnp.where(it > 0, xp_ref[...], 0.0)
    xe_ref[POOL_HALO:POOL_HALO + tm, :] = x
    xe_ref[POOL_HALO + tm:2 * POOL_HALO + tm, :] = jnp.where(it < tiles_per_seq - 1, xn_ref[...], 0.0)
    pos = it * tm + lax.broadcasted_iota(jnp.int32, (tm, 1), 0)
    ys = []
    for gi, win in enumerate(POOL_WINDOWS):
        half = win // 2
        cs = slice(gi * POOL_CH, (gi + 1) * POOL_CH)
        tot = xe_ref[POOL_HALO - half:POOL_HALO - half + tm, cs]
        for o in range(-half + 1, half):
            tot = tot + xe_ref[POOL_HALO + o:POOL_HALO + o + tm, cs]
        cnt = (jnp.minimum(pos + half, seq) - jnp.maximum(pos - half, 0)).astype(jnp.float32)
        u = tot / cnt - x[:, cs]
        ys.append(jnp.dot(u.astype(jnp.bfloat16), w_ref[gi], preferred_element_type=jnp.float32))
    y = jnp.concatenate(ys, axis=1) * sc_ref[...]
    z = DEEPNORM_ALPHA * x + y
    yn = _layer_norm(z, g_ref[...], b_ref[...])
    o_ref[:, 0:D_MODEL] = yn
    _route_tile(yn, wr_ref, rb_ref, base_ref, o_ref.at[:, D_MODEL:ROW_W], meta_ref, cnt_ref)


def _pool_ln(x2, t, w_bf, scale, g, b, wr_t, rbias, seq):
    hb = TM_POOL // POOL_HALO
    nhb = t // POOL_HALO
    r_in, r_out, r_shape, r_scratch = _route_io(t, TM_POOL)
    return pl.pallas_call(
        functools.partial(_poolln_kernel, seq=seq),
        grid=(t // TM_POOL,),
        in_specs=[
            pl.BlockSpec((TM_POOL, D_MODEL), lambda i: (i, 0)),
            pl.BlockSpec((POOL_HALO, D_MODEL), lambda i: (jnp.maximum(i * hb - 1, 0), 0)),
            pl.BlockSpec((POOL_HALO, D_MODEL), lambda i: (jnp.minimum((i + 1) * hb, nhb - 1), 0)),
            pl.BlockSpec(w_bf.shape, lambda i: (0, 0, 0)),
            pl.BlockSpec((1, D_MODEL), lambda i: (0, 0)),
            pl.BlockSpec((1, D_MODEL), lambda i: (0, 0)),
            pl.BlockSpec((1, D_MODEL), lambda i: (0, 0)),
        ] + r_in,
        out_specs=r_out,
        out_shape=r_shape,
        scratch_shapes=[pltpu.VMEM((TM_POOL + 2 * POOL_HALO, D_MODEL), jnp.float32), r_scratch],
        compiler_params=_cparams(("arbitrary",), VMEM_LIMIT),
        name="pool_ln",
    )(x2, x2, x2, w_bf, scale, g, b, wr_t, rbias)


def _route_tile(x, wr_ref, rb_ref, base_ref, gate_ref, meta_ref, cnt_ref):
    tr = x.shape[0]

    @pl.when(pl.program_id(0) == 0)
    def _():
        base_ref[...] = jnp.zeros_like(base_ref)

    xhi = x.astype(jnp.bfloat16)
    xlo = (x - xhi.astype(jnp.float32)).astype(jnp.bfloat16)
    l1 = lax.dot_general(wr_ref[...], xhi, _NT, preferred_element_type=jnp.float32)
    l2 = lax.dot_general(wr_ref[0:N_EXPERTS, :], xlo, _NT, preferred_element_type=jnp.float32)
    logits = l1[0:N_EXPERTS] + l1[N_EXPERTS:2 * N_EXPERTS] + l2
    aff = 1.0 / (1.0 + jnp.exp(-logits))
    sel = aff + rb_ref[...]

    best = jnp.full((1, tr), -jnp.inf, jnp.float32)
    cls = jnp.zeros((1, tr), jnp.int32)
    aa = jnp.zeros((1, tr), jnp.float32)
    ab = jnp.zeros((1, tr), jnp.float32)
    for c in range(N_CLASSES):
        ea, eb = CLASS_EA[c], CLASS_EB[c]
        ps = sel[ea:ea + 1] + sel[eb:eb + 1]
        better = ps > best
        best = jnp.where(better, ps, best)
        cls = jnp.where(better, c, cls)
        aa = jnp.where(better, aff[ea:ea + 1], aa)
        ab = jnp.where(better, aff[eb:eb + 1], ab)
    ga = aa / (aa + ab)
    gb = ab / (aa + ab)

    crow = lax.broadcasted_iota(jnp.int32, (CLASS_ROWS, tr), 0)
    onehot = (crow == cls).astype(jnp.float32)
    r_i = lax.broadcasted_iota(jnp.int32, (tr, tr), 0)
    c_i = lax.broadcasted_iota(jnp.int32, (tr, tr), 1)
    upper = (r_i < c_i).astype(jnp.bfloat16)
    excl = jnp.dot(onehot.astype(jnp.bfloat16), upper, preferred_element_type=jnp.float32)
    base = base_ref[...]
    rank = jnp.sum(onehot * (excl + base[:, 0:1]), axis=0, keepdims=True)
    base = base + jnp.sum(onehot, axis=1, keepdims=True)
    base_ref[...] = base
    cnt_ref[...] = base.astype(jnp.int32)

    mrow = lax.broadcasted_iota(jnp.int32, (8, tr), 0)
    meta_ref[...] = jnp.where(mrow == 0, cls, jnp.where(mrow == 1, rank.astype(jnp.int32), 0))
    grow = lax.broadcasted_iota(jnp.int32, (LANES, tr), 0)
    gmat = jnp.where(grow == 0, ga, jnp.where(grow == 1, gb, 0.0))
    gate_ref[...] = gmat.T


def _route_io(t, tm):
    in_specs = [pl.BlockSpec((2 * N_EXPERTS, D_MODEL), lambda i: (0, 0)),
                pl.BlockSpec((N_EXPERTS, 1), lambda i: (0, 0))]
    out_specs = [pl.BlockSpec((tm, ROW_W), lambda i: (i, 0)),
                 pl.BlockSpec((8, tm), lambda i: (0, i)),
                 pl.BlockSpec((CLASS_ROWS, LANES), lambda i: (0, 0))]
    out_shape = [jax.ShapeDtypeStruct((t, ROW_W), jnp.float32),
                 jax.ShapeDtypeStruct((8, t), jnp.int32),
                 jax.ShapeDtypeStruct((CLASS_ROWS, LANES), jnp.int32)]
    return in_specs, out_specs, out_shape, pltpu.VMEM((CLASS_ROWS, LANES), jnp.float32)


def _row_copy(src_hbm, dst_ref, src_row, dst_row, sem):
    return pltpu.make_async_copy(src_hbm.at[pl.ds(src_row, 1)], dst_ref.at[pl.ds(dst_row, 1)], sem)


def _moe_ffn_kernel(dest_ref, ta_ref, tb_ref, nv_ref, xw_hbm, w1a_ref, w3a_ref, w2a_ref, w1b_ref, w3b_ref, w2b_ref,
                    g_ref, b_ref, o_hbm, xbuf, obuf, xb_ref, h_ref, src_ref, gsem, ssem, *, t_rows, n_tiles):
    del ta_ref, tb_ref
    i = pl.program_id(0)
    nv = nv_ref[0]
    tm = TM_FFN
    chunk = 256
    n_chunks = 2 * D_MODEL // chunk
    rows_per_chunk = tm // n_chunks
    gather_rows = -(-tm // (n_chunks - 2))

    def gather_copy(tile, r, slot):
        return _row_copy(xw_hbm, xbuf.at[slot], jnp.minimum(src_ref[tile * tm + r], t_rows - 1), r, gsem.at[slot])

    def scatter_copy(tile, r, slot):
        return _row_copy(obuf.at[slot], o_hbm, r, src_ref[tile * tm + r], ssem.at[slot])

    def wait_gather(slot):
        pltpu.make_async_copy(xw_hbm.at[pl.ds(0, tm)], xbuf.at[slot], gsem.at[slot]).wait()

    def wait_scatter(slot):
        pltpu.make_async_copy(obuf.at[slot], o_hbm.at[pl.ds(0, tm)], ssem.at[slot]).wait()

    @pl.when(i == 0)
    def _():
        def clear(p, carry):
            src_ref[p] = t_rows + (p & (2 * tm - 1))
            return carry
        lax.fori_loop(0, n_tiles * tm, clear, 0, unroll=16)

        def invert(tok, carry):
            src_ref[dest_ref[tok]] = tok
            return carry
        lax.fori_loop(0, t_rows, invert, 0, unroll=8)

        obuf[...] = jnp.zeros_like(obuf)
        spare = [pltpu.make_async_copy(obuf.at[k % 2], o_hbm.at[pl.ds(t_rows + k * tm, tm)], ssem.at[0])
                 for k in range(ROW_PAD // tm)]
        for cp in spare:
            cp.start()
        for cp in spare:
            cp.wait()

        def issue(r, carry):
            gather_copy(0, r, 0).start()
            return carry
        lax.fori_loop(0, tm, issue, 0, unroll=8)

    @pl.when(i < nv)
    def _():
        slot = i % 2
        other = 1 - slot
        nxt = i + 1
        prv = jnp.where(i == 0, n_tiles - 1, i - 1)
        wait_gather(slot)
        xb_ref[...] = xbuf[slot, :, 0:D_MODEL].astype(jnp.bfloat16)
        ci = 0
        for e, (w1_ref, w3_ref) in enumerate(((w1a_ref, w3a_ref), (w1b_ref, w3b_ref))):
            g = xbuf[slot, :, D_MODEL + e:D_MODEL + e + 1]
            for c in range(D_MODEL // chunk):
                cs = slice(c * chunk, (c + 1) * chunk)
                a = jnp.dot(xb_ref[...], w1_ref[:, cs], preferred_element_type=jnp.float32)
                for r in range(min(ci * gather_rows, tm), min((ci + 1) * gather_rows, tm)):
                    gather_copy(nxt, r, other).start()
                bb = jnp.dot(xb_ref[...], w3_ref[:, cs], preferred_element_type=jnp.float32)
                hcol = (a / (1.0 + jnp.exp(-a))) * bb * g
                h_ref[:, e * D_MODEL + c * chunk:e * D_MODEL + (c + 1) * chunk] = hcol.astype(jnp.bfloat16)
                for r in range(ci * rows_per_chunk, (ci + 1) * rows_per_chunk):
                    scatter_copy(prv, r, other).start()
                ci += 1
        y = jnp.dot(h_ref[:, 0:D_MODEL], w2a_ref[...], preferred_element_type=jnp.float32)
        y = y + jnp.dot(h_ref[:, D_MODEL:2 * D_MODEL], w2b_ref[...], preferred_element_type=jnp.float32)
        z = DEEPNORM_ALPHA * xbuf[slot, :, 0:D_MODEL] + y
        out = _layer_norm(z, g_ref[...], b_ref[...])

        @pl.when(i >= 1)
        def _():
            wait_scatter(slot)

        obuf[slot] = out

        @pl.when(i == nv - 1)
        def _():
            def issue(r, carry):
                scatter_copy(i, r, slot).start()
                return carry
            lax.fori_loop(0, tm, issue, 0, unroll=8)
            wait_scatter(other)
            wait_scatter(slot)
            wait_gather(other)


def _moe_ffn(dest, ta, tb, nvalid, xw, w1, w3, w2, layer, g, b):
    t_rows = dest.shape[0]
    nt = ta.shape[0]

    def wspec(which):
        if which == 0:
            return pl.BlockSpec((None, None, D_MODEL, D_MODEL), lambda i, s, ta, tb, nv: (layer, ta[i], 0, 0))
        return pl.BlockSpec((None, None, D_MODEL, D_MODEL), lambda i, s, ta, tb, nv: (layer, tb[i], 0, 0))

    vec = pl.BlockSpec((1, D_MODEL), lambda i, s, ta, tb, nv: (0, 0))
    return pl.pallas_call(
        functools.partial(_moe_ffn_kernel, t_rows=t_rows, n_tiles=nt),
        grid_spec=pltpu.PrefetchScalarGridSpec(
            num_scalar_prefetch=4,
            grid=(nt,),
            in_specs=[pl.BlockSpec(memory_space=pl.ANY),
                      wspec(0), wspec(0), wspec(0), wspec(1), wspec(1), wspec(1), vec, vec],
            out_specs=pl.BlockSpec(memory_space=pl.ANY),
            scratch_shapes=[pltpu.VMEM((2, TM_FFN, ROW_W), jnp.float32),
                            pltpu.VMEM((2, TM_FFN, D_MODEL), jnp.float32),
                            pltpu.VMEM((TM_FFN, D_MODEL), jnp.bfloat16),
                            pltpu.VMEM((TM_FFN, 2 * D_MODEL), jnp.bfloat16),
                            pltpu.SMEM((nt * TM_FFN,), jnp.int32),
                            pltpu.SemaphoreType.DMA((2,)), pltpu.SemaphoreType.DMA((2,))],
        ),
        out_shape=jax.ShapeDtypeStruct((t_rows + ROW_PAD, D_MODEL), jnp.float32),
        compiler_params=_cparams(("arbitrary",), VMEM_LIMIT),
        name="moe_ffn",
    )(dest, ta, tb, nvalid, xw, w1, w3, w2, w1, w3, w2, g, b)


def _moe_ln(xw, meta, cnt, w1, w3, w2, layer, g, b):
    t = xw.shape[0]
    cls, rank = meta[0], meta[1]
    counts = cnt[:N_CLASSES, 0]
    tiles_c = (counts + TM_FFN - 1) // TM_FFN
    tile_end = jnp.cumsum(tiles_c)
    row_start = (tile_end - tiles_c) * TM_FFN
    dest = row_start[cls] + rank
    nt = t // TM_FFN + N_CLASSES + 1
    nvalid = tile_end[-1]
    tidx = jnp.arange(nt, dtype=jnp.int32)
    tcls = jnp.sum(tile_end[None, :] <= jnp.minimum(tidx, nvalid - 1)[:, None], axis=1).astype(jnp.int32)
    ta = jnp.asarray(CLASS_EA, jnp.int32)[tcls]
    tb = jnp.asarray(CLASS_EB, jnp.int32)[tcls]
    return _moe_ffn(dest, ta, tb, nvalid.reshape(1).astype(jnp.int32), xw, w1, w3, w2, layer, g, b)


def _rope_tables(seq):
    pos = jnp.arange(seq, dtype=jnp.float32)
    inv_freq = ROPE_THETA ** (-jnp.arange(0, HEAD_DIM, 2, dtype=jnp.float32) / HEAD_DIM)
    ang = pos[:, None] * inv_freq[None, :]
    reps = LANES // (HEAD_DIM // 2)
    return jnp.tile(jnp.cos(ang), (1, reps)), jnp.tile(jnp.sin(ang), (1, reps))


def kernel(x, w_in, w_out, rpb, pool_w, pool_scale, router_w, router_bias, moe_w1, moe_w3, moe_w2, ln_g, ln_b):
    b, s, d = x.shape
    t = b * s
    bf = jnp.bfloat16
    cos_t, sin_t = _rope_tables(s)
    perm_np = _residue_perm()
    perm, perm_t = jnp.asarray(perm_np, bf), jnp.asarray(perm_np.T, bf)
    d4_rel = jnp.asarray(_d4_rel())
    rw_hi = router_w.astype(bf)
    rw_lo = (router_w - rw_hi.astype(jnp.float32)).astype(bf)
    wr_t = jnp.concatenate([rw_hi.T, rw_lo.T], axis=0)
    rbias = router_bias.astype(jnp.float32).reshape(N_EXPERTS, 1)
    w_in_bf, w_out_bf, pool_w_bf = w_in.astype(bf), w_out.astype(bf), pool_w.astype(bf)
    w1_bf, w3_bf, w2_bf = moe_w1.astype(bf), moe_w3.astype(bf), moe_w2.astype(bf)

    x2 = x.reshape(t, d)
    for layer in range(DEPTH):
        i = layer // 2
        g0, b0 = ln_g[layer, 0].reshape(1, d), ln_b[layer, 0].reshape(1, d)
        g1, b1 = ln_g[layer, 1].reshape(1, d), ln_b[layer, 1].reshape(1, d)
        if layer % 2 == 0:
            qkv, qres = _qkv_proj(x2, t, w_in_bf[i], cos_t, sin_t, perm, s)
            qkv = qkv.reshape(b, s, 3 * d)
            oa, la = _attn_d16(qres, b, s)
            oa, la = _attn_d4(qres, oa, la, d4_rel, perm_t, b, s)
            oa = _attn_d1(qkv, oa, la)
            ob = _attn_b(qkv, _bias_table(rpb[i]))
            xw, meta, cnt = _out_proj_ln(oa.reshape(t, HALF_D), ob.reshape(t, HALF_D), x2, t, w_out_bf[i], g0, b0,
                                         wr_t, rbias)
        else:
            xw, meta, cnt = _pool_ln(x2, t, pool_w_bf[i], pool_scale[i].reshape(1, d), g0, b0, wr_t, rbias, s)
        x2 = _moe_ln(xw, meta, cnt, w1_bf, w3_bf, w2_bf, layer, g1, b1)
    return x2[:t].reshape(b, s, d)
```

```python
import functools

import numpy as np
import jax
import jax.numpy as jnp
from jax import lax
from jax.experimental import pallas as pl
from jax.experimental.pallas import tpu as pltpu

D_MODEL = 1024
HEAD_DIM = 64
N_HEADS_A = 8
N_HEADS_B = 8
HALF_D = N_HEADS_A * HEAD_DIM
ROPE_THETA = 10000.0
DILATED_PATTERNS = ((128, 1), (512, 4), (2048, 16))
BAND_RADIUS = 64
GRID_W = 64
NA_ROWS = 8
NA_COLS = 16
POOL_WINDOWS = (2, 4, 8, 16)
POOL_CH = D_MODEL // len(POOL_WINDOWS)
POOL_HALO = 8
N_EXPERTS = 16
EXPERTS_PER_GROUP = 4
DEPTH = 4
DEEPNORM_ALPHA = (2.0 * DEPTH) ** 0.25
LN_EPS = 1e-5
NEG_INF = -1e30

LANES = 128
GATE_COLS = LANES
ROW_W = D_MODEL + GATE_COLS

_PAIR_CHAIN = ((0, 1), (0, 2), (1, 2), (1, 3), (0, 3), (2, 3))
CLASS_EA = tuple(g * EXPERTS_PER_GROUP + a for g in range(4) for a, _ in _PAIR_CHAIN)
CLASS_EB = tuple(g * EXPERTS_PER_GROUP + b for g in range(4) for _, b in _PAIR_CHAIN)
N_CLASSES = len(CLASS_EA)
CLASS_ROWS = 32

LT = 512
LR = 16
LJ = LT // LR

TM_QKV = 2048
TN_QKV = 512
TQ_A = 256
TB_ROWS = 8
TM_LN = 512
TM_POOL = 512
TM_FFN = 256
ROW_PAD = 1024
VMEM_LIMIT = 48 * 1024 * 1024

_NT = (((1,), (1,)), ((), ()))


def _cparams(sem, vmem=None):
    return pltpu.CompilerParams(dimension_semantics=sem, vmem_limit_bytes=vmem)


def _layer_norm(z, g, b):
    mu = jnp.mean(z, axis=-1, keepdims=True)
    zc = z - mu
    var = jnp.mean(zc * zc, axis=-1, keepdims=True)
    return zc * lax.rsqrt(var + LN_EPS) * g + b


def _qkv_kernel(x_ref, w_ref, cos_ref, sin_ref, p_ref, o_ref, r_ref, xb_ref):
    n = pl.program_id(1)

    @pl.when(n == 0)
    def _():
        xb_ref[...] = x_ref[...].astype(jnp.bfloat16)

    acc = jnp.dot(xb_ref[...], w_ref[...], preferred_element_type=jnp.float32)
    scale = jnp.where(n < 2, HEAD_DIM ** -0.5, 1.0).astype(jnp.float32)

    @pl.when((n == 0) | (n == 2))
    def _():
        cos = cos_ref[...]
        sin = sin_ref[...]
        lane = lax.broadcasted_iota(jnp.int32, cos.shape, 1)
        first = (lane % HEAD_DIM) < (HEAD_DIM // 2)
        for c in range(TN_QKV // LANES):
            a = acc[:, c * LANES:(c + 1) * LANES]
            nxt = pltpu.roll(a, LANES - HEAD_DIM // 2, 1)
            prv = pltpu.roll(a, HEAD_DIM // 2, 1)
            rot = a * cos + jnp.where(first, -nxt, prv) * sin
            o_ref[:, c * LANES:(c + 1) * LANES] = (rot * scale).astype(o_ref.dtype)

    @pl.when((n != 0) & (n != 2))
    def _():
        o_ref[...] = (acc * scale).astype(o_ref.dtype)

    @pl.when(n % 2 == 0)
    def _():
        for hlf in range(TM_QKV // LT):
            rs = slice(hlf * LT, (hlf + 1) * LT)
            r_ref[rs, :] = jnp.dot(p_ref[...], o_ref[rs, :], preferred_element_type=jnp.float32).astype(r_ref.dtype)


def _residue_perm():
    p = np.zeros((LT, LT), np.float32)
    j, r = np.meshgrid(np.arange(LJ), np.arange(LR))
    p[(r * LJ + j).ravel(), (LR * j + r).ravel()] = 1.0
    return p


def _qkv_proj(x2, t, w_bf, cos_t, sin_t, perm, seq):
    tiles_per_seq = seq // TM_QKV
    return pl.pallas_call(
        _qkv_kernel,
        grid=(t // TM_QKV, 3 * D_MODEL // TN_QKV),
        in_specs=[
            pl.BlockSpec((TM_QKV, D_MODEL), lambda i, n: (i, 0)),
            pl.BlockSpec((D_MODEL, TN_QKV), lambda i, n: (0, n)),
            pl.BlockSpec((TM_QKV, LANES), lambda i, n: (i % tiles_per_seq, 0)),
            pl.BlockSpec((TM_QKV, LANES), lambda i, n: (i % tiles_per_seq, 0)),
            pl.BlockSpec((LT, LT), lambda i, n: (0, 0)),
        ],
        out_specs=[pl.BlockSpec((TM_QKV, TN_QKV), lambda i, n: (i, n)),
                   pl.BlockSpec((TM_QKV, HALF_D), lambda i, n: (i, n // 2))],
        out_shape=[jax.ShapeDtypeStruct((t, 3 * D_MODEL), jnp.bfloat16),
                   jax.ShapeDtypeStruct((t, 3 * HALF_D), jnp.bfloat16)],
        scratch_shapes=[pltpu.VMEM((TM_QKV, D_MODEL), jnp.bfloat16)],
        compiler_params=_cparams(("parallel", "arbitrary"), VMEM_LIMIT),
        name="qkv_proj",
    )(x2, w_bf, cos_t, sin_t, perm)


def _band_block(q, kcat, vcat, valid, o_prev, l_prev):
    tq = q.shape[0]
    lane = lax.broadcasted_iota(jnp.int32, (tq, LANES), 1)
    lo = lane < HEAD_DIM
    valid2 = jnp.concatenate([valid, valid], axis=0)
    m_tile = jnp.zeros((tq, LANES), jnp.float32)
    s_tile = jnp.ones((tq, LANES), jnp.float32)
    pair_out = []
    for hp in range(N_HEADS_A // 2):
        sl = slice(hp * LANES, (hp + 1) * LANES)
        q2, k2, v2 = q[:, sl], kcat[:, sl], vcat[:, sl]
        zero = jnp.zeros_like(q2)
        qs = jnp.concatenate([jnp.where(lo, q2, zero), jnp.where(lo, zero, q2)], axis=0)
        s = lax.dot_general(qs, k2, _NT, preferred_element_type=jnp.float32)
        s = jnp.where(valid2, s, NEG_INF)
        m = jnp.max(s, axis=1, keepdims=True)
        p = jnp.exp(s - m)
        ssum = jnp.sum(p, axis=1, keepdims=True)
        o2 = jnp.dot(p.astype(jnp.bfloat16), v2, preferred_element_type=jnp.float32)
        pair_out.append(jnp.where(lo, o2[0:tq], o2[tq:2 * tq]))
        m_tile = jnp.where(lane == 2 * hp, m[0:tq], jnp.where(lane == 2 * hp + 1, m[tq:2 * tq], m_tile))
        s_tile = jnp.where(lane == 2 * hp, ssum[0:tq], jnp.where(lane == 2 * hp + 1, ssum[tq:2 * tq], s_tile))

    lse = m_tile + jnp.log(s_tile)
    if l_prev is None:
        c_new = 1.0 / s_tile
        l_new = lse
    else:
        m2 = jnp.maximum(l_prev, lse)
        wp = jnp.exp(l_prev - m2)
        wn = jnp.exp(lse - m2)
        den = wp + wn
        l_new = m2 + jnp.log(den)
        c_prev = wp / den
        c_new = wn / (den * s_tile)
    outs = []
    for hp in range(N_HEADS_A // 2):
        sl = slice(hp * LANES, (hp + 1) * LANES)
        h0 = 2 * hp
        out = jnp.where(lo, c_new[:, h0:h0 + 1], c_new[:, h0 + 1:h0 + 2]) * pair_out[hp]
        if l_prev is not None:
            cp2 = jnp.where(lo, c_prev[:, h0:h0 + 1], c_prev[:, h0 + 1:h0 + 2])
            out = out + cp2 * o_prev[:, sl].astype(jnp.float32)
        outs.append(out)
    return outs, l_new


def _band_valid(tq, j, nblk):
    tk = tq + 2 * BAND_RADIUS
    row = lax.broadcasted_iota(jnp.int32, (tq, tk), 0)
    col = lax.broadcasted_iota(jnp.int32, (tq, tk), 1)
    rel = col - row
    valid = (rel >= 0) & (rel <= 2 * BAND_RADIUS)
    valid = valid & ((col >= BAND_RADIUS) | (j > 0))
    return valid & ((col < tq + BAND_RADIUS) | (j < nblk - 1))


def _attn_d1_kernel(q_ref, kl_ref, km_ref, kr_ref, vl_ref, vm_ref, vr_ref, op_ref, lp_ref, o_ref, *, nblk):
    tq = q_ref.shape[0]
    valid = _band_valid(tq, pl.program_id(1), nblk)
    kcat = jnp.concatenate([kl_ref[...], km_ref[...], kr_ref[...]], axis=0)
    vcat = jnp.concatenate([vl_ref[...], vm_ref[...], vr_ref[...]], axis=0)
    outs, _ = _band_block(q_ref[...], kcat, vcat, valid, op_ref[...], lp_ref[...])
    for hp, out in enumerate(outs):
        o_ref[:, hp * LANES:(hp + 1) * LANES] = out.astype(o_ref.dtype)


def _attn_d16_kernel(q_ref, kl_ref, km_ref, kr_ref, vl_ref, vm_ref, vr_ref, o_ref, l_ref, *, nblk):
    def rows(ref):
        return jnp.concatenate([ref[a] for a in range(ref.shape[0])], axis=0)

    nq = q_ref.shape[0]
    tq = nq * LJ
    valid = _band_valid(tq, pl.program_id(2), nblk)
    kcat = jnp.concatenate([rows(kl_ref), rows(km_ref), rows(kr_ref)], axis=0)
    vcat = jnp.concatenate([rows(vl_ref), rows(vm_ref), rows(vr_ref)], axis=0)
    outs, l_new = _band_block(rows(q_ref), kcat, vcat, valid, None, None)
    for a in range(nq):
        for hp, out in enumerate(outs):
            o_ref[a, :, hp * LANES:(hp + 1) * LANES] = out[a * LJ:(a + 1) * LJ].astype(o_ref.dtype)
        l_ref[a] = l_new[a * LJ:(a + 1) * LJ]


def _d4_rel():
    half = LJ // 2
    sub_tile = LT // 4
    mq = np.array([4 * j + q for q in range(4) for j in range(LJ)])
    prev = np.array([4 * (half + j) + q - sub_tile for q in range(4) for j in range(half)])
    nxt = np.array([sub_tile + 4 * j + q for q in range(4) for j in range(half)])
    mk = np.concatenate([prev, mq, nxt])
    return (mk[None, :] - mq[:, None]).astype(np.int32)


def _attn_d4_kernel(q_ref, kp_ref, km_ref, kn_ref, vp_ref, vm_ref, vn_ref, op_ref, lp_ref, rel_ref, pt_ref,
                    o_ref, l_ref, ores_ref, lres_ref, *, ntiles):
    t = pl.program_id(1)
    half = LJ // 2
    nsub = LT // 4
    rel = rel_ref[...]
    col = lax.broadcasted_iota(jnp.int32, rel.shape, 1)
    valid = (rel >= -BAND_RADIUS) & (rel <= BAND_RADIUS)
    valid = valid & ((col >= 4 * half) | (t > 0)) & ((col < 4 * half + nsub) | (t < ntiles - 1))
    for r4 in range(4):
        res = [4 * q + r4 for q in range(4)]

        def main(ref):
            return jnp.concatenate([ref[r * LJ:(r + 1) * LJ, :] for r in res], axis=0)

        def window(p_ref, m_ref, n_ref):
            return jnp.concatenate([p_ref[r] for r in res] + [main(m_ref)] + [n_ref[r] for r in res], axis=0)

        outs, l_new = _band_block(main(q_ref), window(kp_ref, km_ref, kn_ref), window(vp_ref, vm_ref, vn_ref),
                                  valid, main(op_ref), main(lp_ref))
        for qi, r in enumerate(res):
            for hp, out in enumerate(outs):
                ores_ref[r * LJ:(r + 1) * LJ, hp * LANES:(hp + 1) * LANES] = (
                    out[qi * LJ:(qi + 1) * LJ].astype(ores_ref.dtype))
            lres_ref[r * LJ:(r + 1) * LJ, :] = l_new[qi * LJ:(qi + 1) * LJ]

    pt = pt_ref[...]
    o_ref[...] = jnp.dot(pt, ores_ref[...], preferred_element_type=jnp.float32).astype(o_ref.dtype)
    lres = lres_ref[...]
    l_a = lres.astype(jnp.bfloat16)
    rem = lres - l_a.astype(jnp.float32)
    l_b = rem.astype(jnp.bfloat16)
    l_c = (rem - l_b.astype(jnp.float32)).astype(jnp.bfloat16)
    l_ref[...] = (jnp.dot(pt, l_a, preferred_element_type=jnp.float32)
                  + jnp.dot(pt, l_b, preferred_element_type=jnp.float32)
                  + jnp.dot(pt, l_c, preferred_element_type=jnp.float32))


def _attn_d16(qres, b, s):
    nt = s // LT
    nq = TQ_A // LJ
    nh = BAND_RADIUS // LJ
    nblk = nt // nq
    qv = qres.reshape(b, nt, LR, LJ, 3 * HALF_D)

    def main_spec(cb, width=HALF_D):
        return pl.BlockSpec((None, nq, None, LJ, width), lambda bi, r, j: (bi, j, r, 0, cb))

    def left_spec(cb):
        return pl.BlockSpec((None, nh, None, LJ, HALF_D),
                            lambda bi, r, j: (bi, jnp.maximum(j * (nq // nh) - 1, 0), r, 0, cb))

    def right_spec(cb):
        return pl.BlockSpec((None, nh, None, LJ, HALF_D),
                            lambda bi, r, j: (bi, jnp.minimum((j + 1) * (nq // nh), nt // nh - 1), r, 0, cb))

    return pl.pallas_call(
        functools.partial(_attn_d16_kernel, nblk=nblk),
        grid=(b, LR, nblk),
        in_specs=[main_spec(0), left_spec(1), main_spec(1), right_spec(1), left_spec(2), main_spec(2), right_spec(2)],
        out_specs=[main_spec(0), main_spec(0, LANES)],
        out_shape=[jax.ShapeDtypeStruct((b, nt, LR, LJ, HALF_D), jnp.bfloat16),
                   jax.ShapeDtypeStruct((b, nt, LR, LJ, LANES), jnp.float32)],
        compiler_params=_cparams(("parallel", "parallel", "parallel"), VMEM_LIMIT),
        name="attn_dilated_16",
    )(*([qv] * 7))


def _attn_d4(qres, o_prev, l_prev, rel, perm_t, b, s):
    nt = s // LT
    half = LJ // 2
    qv = qres.reshape(b, nt, LT, 3 * HALF_D)
    qh = qres.reshape(b, nt, LR, 2, half, 3 * HALF_D)

    def tile_spec(cb, width=HALF_D):
        return pl.BlockSpec((None, None, LT, width), lambda bi, t: (bi, t, 0, cb))

    def prev_spec(cb):
        return pl.BlockSpec((None, None, LR, None, half, HALF_D),
                            lambda bi, t: (bi, jnp.maximum(t - 1, 0), 0, 1, 0, cb))

    def next_spec(cb):
        return pl.BlockSpec((None, None, LR, None, half, HALF_D),
                            lambda bi, t: (bi, jnp.minimum(t + 1, nt - 1), 0, 0, 0, cb))

    return pl.pallas_call(
        functools.partial(_attn_d4_kernel, ntiles=nt),
        grid=(b, nt),
        in_specs=[tile_spec(0), prev_spec(1), tile_spec(1), next_spec(1), prev_spec(2), tile_spec(2), next_spec(2),
                  tile_spec(0), tile_spec(0, LANES),
                  pl.BlockSpec(rel.shape, lambda bi, t: (0, 0)), pl.BlockSpec((LT, LT), lambda bi, t: (0, 0))],
        out_specs=[pl.BlockSpec((None, LT, HALF_D), lambda bi, t: (bi, t, 0)),
                   pl.BlockSpec((None, LT, LANES), lambda bi, t: (bi, t, 0))],
        out_shape=[jax.ShapeDtypeStruct((b, s, HALF_D), jnp.bfloat16),
                   jax.ShapeDtypeStruct((b, s, LANES), jnp.float32)],
        scratch_shapes=[pltpu.VMEM((LT, HALF_D), jnp.bfloat16), pltpu.VMEM((LT, LANES), jnp.float32)],
        compiler_params=_cparams(("parallel", "parallel"), VMEM_LIMIT),
        name="attn_dilated_4",
    )(qv, qh, qv, qh, qh, qv, qh, o_prev.reshape(b, nt, LT, HALF_D), l_prev.reshape(b, nt, LT, LANES), rel, perm_t)


def _attn_d1(qkv, o_prev, l_prev):
    b, s, _ = qkv.shape
    tq = TQ_A
    nblk = s // tq
    per = tq // BAND_RADIUS
    nhalo = s // BAND_RADIUS

    def main_spec(cb, width=HALF_D):
        return pl.BlockSpec((None, tq, width), lambda bi, j: (bi, j, cb))

    def left_spec(cb):
        return pl.BlockSpec((None, BAND_RADIUS, HALF_D), lambda bi, j: (bi, jnp.maximum(j * per - 1, 0), cb))

    def right_spec(cb):
        return pl.BlockSpec((None, BAND_RADIUS, HALF_D),
                            lambda bi, j: (bi, jnp.minimum((j + 1) * per, nhalo - 1), cb))

    return pl.pallas_call(
        functools.partial(_attn_d1_kernel, nblk=nblk),
        grid=(b, nblk),
        in_specs=[main_spec(0), left_spec(2), main_spec(2), right_spec(2), left_spec(4), main_spec(4), right_spec(4),
                  main_spec(0), main_spec(0, LANES)],
        out_specs=main_spec(0),
        out_shape=jax.ShapeDtypeStruct((b, s, HALF_D), jnp.bfloat16),
        compiler_params=_cparams(("parallel", "parallel"), VMEM_LIMIT),
        name="attn_dilated_1",
    )(qkv, qkv, qkv, qkv, qkv, qkv, qkv, o_prev, l_prev)


def _attn_b_kernel(q_ref, kp_ref, kc_ref, kn_ref, vp_ref, vc_ref, vn_ref, t2_ref, o_ref, kw_ref, vw_ref, *, rows):
    ib = pl.program_id(1)
    blk = TB_ROWS * GRID_W
    kw_ref[0:blk] = kp_ref[...]
    kw_ref[blk:2 * blk] = kc_ref[...]
    kw_ref[2 * blk:3 * blk] = kn_ref[...]
    vw_ref[0:blk] = vp_ref[...]
    vw_ref[blk:2 * blk] = vc_ref[...]
    vw_ref[2 * blk:3 * blk] = vn_ref[...]
    lane = lax.broadcasted_iota(jnp.int32, (GRID_W, LANES), 1)
    lo = lane < HEAD_DIM
    nkeys = NA_ROWS * GRID_W

    for hp in range(N_HEADS_B // 2):
        sl = slice(hp * LANES, (hp + 1) * LANES)

        def body(t, carry, sl=sl, hp=hp):
            i = ib * TB_ROWS + t
            rs = jnp.clip(i - NA_ROWS // 2, 0, rows - NA_ROWS)
            off = pl.multiple_of((rs - ib * TB_ROWS + TB_ROWS) * GRID_W, GRID_W)
            dlt = rs - i + (NA_ROWS - 1)
            qrow = pl.multiple_of(t * GRID_W, GRID_W)
            q2 = q_ref[pl.ds(qrow, GRID_W), sl]
            k2 = kw_ref[pl.ds(off, nkeys), sl]
            v2 = vw_ref[pl.ds(off, nkeys), sl]
            zero = jnp.zeros_like(q2)
            qs = jnp.concatenate([jnp.where(lo, q2, zero), jnp.where(lo, zero, q2)], axis=0)
            s = lax.dot_general(qs, k2, _NT, preferred_element_type=jnp.float32)
            bias = jnp.concatenate(
                [jnp.concatenate([t2_ref[hp * 2 + hh, dlt + 2 * kp] for kp in range(NA_ROWS // 2)], axis=1)
                 for hh in range(2)], axis=0)
            s = s + bias
            m = jnp.max(s, axis=1, keepdims=True)
            p = jnp.exp(s - m)
            ssum = jnp.sum(p, axis=1, keepdims=True)
            o2 = jnp.dot(p.astype(jnp.bfloat16), v2, preferred_element_type=jnp.float32) / ssum
            o_ref[pl.ds(qrow, GRID_W), sl] = jnp.where(lo, o2[0:GRID_W], o2[GRID_W:2 * GRID_W]).astype(o_ref.dtype)
            return carry

        lax.fori_loop(0, TB_ROWS, body, 0, unroll=True)


def _bias_table(rpb):
    c = np.arange(GRID_W)
    cstart = np.clip(c - NA_COLS // 2, 0, GRID_W - NA_COLS)
    colmask = (c[None, :] >= cstart[:, None]) & (c[None, :] < cstart[:, None] + NA_COLS)
    coff = np.clip(c[None, :] - c[:, None], 1 - NA_COLS, NA_COLS - 1) + (NA_COLS - 1)
    bc = rpb[:, :, coff].astype(jnp.float32)
    bc = jnp.where(colmask[None, None], bc, NEG_INF)
    return jnp.concatenate([bc[:, :-1], bc[:, 1:]], axis=-1)


def _attn_b(qkv, t2):
    b, s, _ = qkv.shape
    rows = s // GRID_W
    blk = TB_ROWS * GRID_W
    nb = s // blk

    def spec(cb, shift):
        return pl.BlockSpec((None, blk, HALF_D),
                            lambda bi, ib: (bi, jnp.clip(ib + shift, 0, nb - 1), cb))

    return pl.pallas_call(
        functools.partial(_attn_b_kernel, rows=rows),
        grid=(b, nb),
        in_specs=[spec(1, 0), spec(3, -1), spec(3, 0), spec(3, 1), spec(5, -1), spec(5, 0), spec(5, 1),
                  pl.BlockSpec(t2.shape, lambda bi, ib: (0, 0, 0, 0))],
        out_specs=pl.BlockSpec((None, blk, HALF_D), lambda bi, ib: (bi, ib, 0)),
        out_shape=jax.ShapeDtypeStruct((b, s, HALF_D), jnp.bfloat16),
        scratch_shapes=[pltpu.VMEM((3 * blk, HALF_D), jnp.bfloat16), pltpu.VMEM((3 * blk, HALF_D), jnp.bfloat16)],
        compiler_params=_cparams(("parallel", "parallel"), VMEM_LIMIT),
        name="attn_neighbourhood",
    )(qkv, qkv, qkv, qkv, qkv, qkv, qkv, t2)


def _outln_kernel(oa_ref, ob_ref, x_ref, w_ref, g_ref, b_ref, wr_ref, rb_ref, o_ref, meta_ref, cnt_ref, base_ref):
    h = jnp.dot(oa_ref[...], w_ref[0:HALF_D, :], preferred_element_type=jnp.float32)
    h = h + jnp.dot(ob_ref[...], w_ref[HALF_D:D_MODEL, :], preferred_element_type=jnp.float32)
    z = DEEPNORM_ALPHA * x_ref[...] + h
    y = _layer_norm(z, g_ref[...], b_ref[...])
    o_ref[:, 0:D_MODEL] = y
    _route_tile(y, wr_ref, rb_ref, base_ref, o_ref.at[:, D_MODEL:ROW_W], meta_ref, cnt_ref)


def _out_proj_ln(oa, ob, x2, t, w_bf, g, b, wr_t, rbias):
    r_in, r_out, r_shape, r_scratch = _route_io(t, TM_LN)
    return pl.pallas_call(
        _outln_kernel,
        grid=(t // TM_LN,),
        in_specs=[
            pl.BlockSpec((TM_LN, HALF_D), lambda i: (i, 0)),
            pl.BlockSpec((TM_LN, HALF_D), lambda i: (i, 0)),
            pl.BlockSpec((TM_LN, D_MODEL), lambda i: (i, 0)),
            pl.BlockSpec((D_MODEL, D_MODEL), lambda i: (0, 0)),
            pl.BlockSpec((1, D_MODEL), lambda i: (0, 0)),
            pl.BlockSpec((1, D_MODEL), lambda i: (0, 0)),
        ] + r_in,
        out_specs=r_out,
        out_shape=r_shape,
        scratch_shapes=[r_scratch],
        compiler_params=_cparams(("arbitrary",), VMEM_LIMIT),
        name="out_proj_ln",
    )(oa, ob, x2, w_bf, g, b, wr_t, rbias)


def _poolln_kernel(xm_ref, xp_ref, xn_ref, w_ref, sc_ref, g_ref, b_ref, wr_ref, rb_ref,
                   o_ref, meta_ref, cnt_ref, xe_ref, base_ref, *, seq):
    i = pl.program_id(0)
    tm = xm_ref.shape[0]
    tiles_per_seq = seq // tm
    it = i % tiles_per_seq
    x = xm_ref[...]
    xe_ref[0:POOL_HALO, :] = jnp.where(it > 0, xp_ref[...], 0.0)
    xe_ref[POOL_HALO:POOL_HALO + tm, :] = x
    xe_ref[POOL_HALO + tm:2 * POOL_HALO + tm, :] = jnp.where(it < tiles_per_seq - 1, xn_ref[...], 0.0)
    pos = it * tm + lax.broadcasted_iota(jnp.int32, (tm, 1), 0)
    ys = []
    for gi, win in enumerate(POOL_WINDOWS):
        half = win // 2
        cs = slice(gi * POOL_CH, (gi + 1) * POOL_CH)
        tot = xe_ref[POOL_HALO - half:POOL_HALO - half + tm, cs]
        for o in range(-half + 1, half):
            tot = tot + xe_ref[POOL_HALO + o:POOL_HALO + o + tm, cs]
        cnt = (jnp.minimum(pos + half, seq) - jnp.maximum(pos - half, 0)).astype(jnp.float32)
        u = tot / cnt - x[:, cs]
        ys.append(jnp.dot(u.astype(jnp.bfloat16), w_ref[gi], preferred_element_type=jnp.float32))
    y = jnp.concatenate(ys, axis=1) * sc_ref[...]
    z = DEEPNORM_ALPHA * x + y
    yn = _layer_norm(z, g_ref[...], b_ref[...])
    o_ref[:, 0:D_MODEL] = yn
    _route_tile(yn, wr_ref, rb_ref, base_ref, o_ref.at[:, D_MODEL:ROW_W], meta_ref, cnt_ref)


def _pool_ln(x2, t, w_bf, scale, g, b, wr_t, rbias, seq):
    hb = TM_POOL // POOL_HALO
    nhb = t // POOL_HALO
    r_in, r_out, r_shape, r_scratch = _route_io(t, TM_POOL)
    return pl.pallas_call(
        functools.partial(_poolln_kernel, seq=seq),
        grid=(t // TM_POOL,),
        in_specs=[
            pl.BlockSpec((TM_POOL, D_MODEL), lambda i: (i, 0)),
            pl.BlockSpec((POOL_HALO, D_MODEL), lambda i: (jnp.maximum(i * hb - 1, 0), 0)),
            pl.BlockSpec((POOL_HALO, D_MODEL), lambda i: (jnp.minimum((i + 1) * hb, nhb - 1), 0)),
            pl.BlockSpec(w_bf.shape, lambda i: (0, 0, 0)),
            pl.BlockSpec((1, D_MODEL), lambda i: (0, 0)),
            pl.BlockSpec((1, D_MODEL), lambda i: (0, 0)),
            pl.BlockSpec((1, D_MODEL), lambda i: (0, 0)),
        ] + r_in,
        out_specs=r_out,
        out_shape=r_shape,
        scratch_shapes=[pltpu.VMEM((TM_POOL + 2 * POOL_HALO, D_MODEL), jnp.float32), r_scratch],
        compiler_params=_cparams(("arbitrary",), VMEM_LIMIT),
        name="pool_ln",
    )(x2, x2, x2, w_bf, scale, g, b, wr_t, rbias)


def _route_tile(x, wr_ref, rb_ref, base_ref, gate_ref, meta_ref, cnt_ref):
    tr = x.shape[0]

    @pl.when(pl.program_id(0) == 0)
    def _():
        base_ref[...] = jnp.zeros_like(base_ref)

    xhi = x.astype(jnp.bfloat16)
    xlo = (x - xhi.astype(jnp.float32)).astype(jnp.bfloat16)
    l1 = lax.dot_general(wr_ref[...], xhi, _NT, preferred_element_type=jnp.float32)
    l2 = lax.dot_general(wr_ref[0:N_EXPERTS, :], xlo, _NT, preferred_element_type=jnp.float32)
    logits = l1[0:N_EXPERTS] + l1[N_EXPERTS:2 * N_EXPERTS] + l2
    aff = 1.0 / (1.0 + jnp.exp(-logits))
    sel = aff + rb_ref[...]

    best = jnp.full((1, tr), -jnp.inf, jnp.float32)
    cls = jnp.zeros((1, tr), jnp.int32)
    aa = jnp.zeros((1, tr), jnp.float32)
    ab = jnp.zeros((1, tr), jnp.float32)
    for c in range(N_CLASSES):
        ea, eb = CLASS_EA[c], CLASS_EB[c]
        ps = sel[ea:ea + 1] + sel[eb:eb + 1]
        better = ps > best
        best = jnp.where(better, ps, best)
        cls = jnp.where(better, c, cls)
        aa = jnp.where(better, aff[ea:ea + 1], aa)
        ab = jnp.where(better, aff[eb:eb + 1], ab)
    ga = aa / (aa + ab)
    gb = ab / (aa + ab)

    crow = lax.broadcasted_iota(jnp.int32, (CLASS_ROWS, tr), 0)
    onehot = (crow == cls).astype(jnp.float32)
    r_i = lax.broadcasted_iota(jnp.int32, (tr, tr), 0)
    c_i = lax.broadcasted_iota(jnp.int32, (tr, tr), 1)
    upper = (r_i < c_i).astype(jnp.bfloat16)
    excl = jnp.dot(onehot.astype(jnp.bfloat16), upper, preferred_element_type=jnp.float32)
    base = base_ref[...]
    rank = jnp.sum(onehot * (excl + base[:, 0:1]), axis=0, keepdims=True)
    base = base + jnp.sum(onehot, axis=1, keepdims=True)
    base_ref[...] = base
    cnt_ref[...] = base.astype(jnp.int32)

    mrow = lax.broadcasted_iota(jnp.int32, (8, tr), 0)
    meta_ref[...] = jnp.where(mrow == 0, cls, jnp.where(mrow == 1, rank.astype(jnp.int32), 0))
    grow = lax.broadcasted_iota(jnp.int32, (LANES, tr), 0)
    gmat = jnp.where(grow == 0, ga, jnp.where(grow == 1, gb, 0.0))
    gate_ref[...] = gmat.T


def _route_io(t, tm):
    in_specs = [pl.BlockSpec((2 * N_EXPERTS, D_MODEL), lambda i: (0, 0)),
                pl.BlockSpec((N_EXPERTS, 1), lambda i: (0, 0))]
    out_specs = [pl.BlockSpec((tm, ROW_W), lambda i: (i, 0)),
                 pl.BlockSpec((8, tm), lambda i: (0, i)),
                 pl.BlockSpec((CLASS_ROWS, LANES), lambda i: (0, 0))]
    out_shape = [jax.ShapeDtypeStruct((t, ROW_W), jnp.float32),
                 jax.ShapeDtypeStruct((8, t), jnp.int32),
                 jax.ShapeDtypeStruct((CLASS_ROWS, LANES), jnp.int32)]
    return in_specs, out_specs, out_shape, pltpu.VMEM((CLASS_ROWS, LANES), jnp.float32)


def _row_copy(src_hbm, dst_ref, src_row, dst_row, sem):
    return pltpu.make_async_copy(src_hbm.at[pl.ds(src_row, 1)], dst_ref.at[pl.ds(dst_row, 1)], sem)


def _moe_ffn_kernel(dest_ref, ta_ref, tb_ref, nv_ref, xw_hbm, w1a_ref, w3a_ref, w2a_ref, w1b_ref, w3b_ref, w2b_ref,
                    g_ref, b_ref, o_hbm, xbuf, obuf, xb_ref, h_ref, src_ref, gsem, ssem, *, t_rows, n_tiles):
    del ta_ref, tb_ref
    i = pl.program_id(0)
    nv = nv_ref[0]
    tm = TM_FFN
    chunk = 256
    rows_per_chunk = tm * chunk // (2 * D_MODEL)

    def gather_copy(tile, r, slot):
        return _row_copy(xw_hbm, xbuf.at[slot], jnp.minimum(src_ref[tile * tm + r], t_rows - 1), r, gsem.at[slot])

    def scatter_copy(tile, r, slot):
        return _row_copy(obuf.at[slot], o_hbm, r, src_ref[tile * tm + r], ssem.at[slot])

    def wait_gather(slot):
        pltpu.make_async_copy(xw_hbm.at[pl.ds(0, tm)], xbuf.at[slot], gsem.at[slot]).wait()

    def wait_scatter(slot):
        pltpu.make_async_copy(obuf.at[slot], o_hbm.at[pl.ds(0, tm)], ssem.at[slot]).wait()

    @pl.when(i == 0)
    def _():
        def clear(k, carry):
            base = t_rows + (k & 1) * tm
            for r in range(tm):
                src_ref[k * tm + r] = base + r
            return carry
        lax.fori_loop(0, n_tiles, clear, 0)

        def invert(tok, carry):
            src_ref[dest_ref[tok]] = tok
            return carry
        lax.fori_loop(0, t_rows, invert, 0, unroll=8)

        obuf[...] = jnp.zeros_like(obuf)
        spare = [pltpu.make_async_copy(obuf.at[k % 2], o_hbm.at[pl.ds(t_rows + k * tm, tm)], ssem.at[0])
                 for k in range(ROW_PAD // tm)]
        for cp in spare:
            cp.start()
        for cp in spare:
            cp.wait()

        def issue(r, carry):
            gather_copy(0, r, 0).start()
            return carry
        lax.fori_loop(0, tm, issue, 0, unroll=8)

    @pl.when(i < nv)
    def _():
        slot = i % 2
        other = 1 - slot
        nxt = i + 1
        prv = jnp.where(i == 0, n_tiles - 1, i - 1)
        wait_gather(slot)
        xb_ref[...] = xbuf[slot, :, 0:D_MODEL].astype(jnp.bfloat16)
        ci = 0
        for e, (w1_ref, w3_ref) in enumerate(((w1a_ref, w3a_ref), (w1b_ref, w3b_ref))):
            g = xbuf[slot, :, D_MODEL + e:D_MODEL + e + 1]
            for c in range(D_MODEL // chunk):
                cs = slice(c * chunk, (c + 1) * chunk)
                a = jnp.dot(xb_ref[...], w1_ref[:, cs], preferred_element_type=jnp.float32)
                bb = jnp.dot(xb_ref[...], w3_ref[:, cs], preferred_element_type=jnp.float32)
                hcol = (a / (1.0 + jnp.exp(-a))) * bb * g
                h_ref[:, e * D_MODEL + c * chunk:e * D_MODEL + (c + 1) * chunk] = hcol.astype(jnp.bfloat16)
                for r in range(ci * rows_per_chunk, (ci + 1) * rows_per_chunk):
                    gather_copy(nxt, r, other).start()
                    scatter_copy(prv, r, other).start()
                ci += 1
        y = jnp.dot(h_ref[:, 0:D_MODEL], w2a_ref[...], preferred_element_type=jnp.float32)
        y = y + jnp.dot(h_ref[:, D_MODEL:2 * D_MODEL], w2b_ref[...], preferred_element_type=jnp.float32)
        z = DEEPNORM_ALPHA * xbuf[slot, :, 0:D_MODEL] + y
        out = _layer_norm(z, g_ref[...], b_ref[...])

        @pl.when(i >= 1)
        def _():
            wait_scatter(slot)

        obuf[slot] = out

        @pl.when(i == nv - 1)
        def _():
            def issue(r, carry):
                scatter_copy(i, r, slot).start()
                return carry
            lax.fori_loop(0, tm, issue, 0, unroll=8)
            wait_scatter(other)
            wait_scatter(slot)
            wait_gather(other)


def _moe_ffn(dest, ta, tb, nvalid, xw, w1, w3, w2, layer, g, b):
    t_rows = dest.shape[0]
    nt = ta.shape[0]

    def wspec(which):
        if which == 0:
            return pl.BlockSpec((None, None, D_MODEL, D_MODEL), lambda i, s, ta, tb, nv: (layer, ta[i], 0, 0))
        return pl.BlockSpec((None, None, D_MODEL, D_MODEL), lambda i, s, ta, tb, nv: (layer, tb[i], 0, 0))

    vec = pl.BlockSpec((1, D_MODEL), lambda i, s, ta, tb, nv: (0, 0))
    return pl.pallas_call(
        functools.partial(_moe_ffn_kernel, t_rows=t_rows, n_tiles=nt),
        grid_spec=pltpu.PrefetchScalarGridSpec(
            num_scalar_prefetch=4,
            grid=(nt,),
            in_specs=[pl.BlockSpec(memory_space=pl.ANY),
                      wspec(0), wspec(0), wspec(0), wspec(1), wspec(1), wspec(1), vec, vec],
            out_specs=pl.BlockSpec(memory_space=pl.ANY),
            scratch_shapes=[pltpu.VMEM((2, TM_FFN, ROW_W), jnp.float32),
                            pltpu.VMEM((2, TM_FFN, D_MODEL), jnp.float32),
                            pltpu.VMEM((TM_FFN, D_MODEL), jnp.bfloat16),
                            pltpu.VMEM((TM_FFN, 2 * D_MODEL), jnp.bfloat16),
                            pltpu.SMEM((nt * TM_FFN,), jnp.int32),
                            pltpu.SemaphoreType.DMA((2,)), pltpu.SemaphoreType.DMA((2,))],
        ),
        out_shape=jax.ShapeDtypeStruct((t_rows + ROW_PAD, D_MODEL), jnp.float32),
        compiler_params=_cparams(("arbitrary",), VMEM_LIMIT),
        name="moe_ffn",
    )(dest, ta, tb, nvalid, xw, w1, w3, w2, w1, w3, w2, g, b)


def _moe_ln(xw, meta, cnt, w1, w3, w2, layer, g, b):
    t = xw.shape[0]
    cls, rank = meta[0], meta[1]
    counts = cnt[:N_CLASSES, 0]
    tiles_c = (counts + TM_FFN - 1) // TM_FFN
    tile_end = jnp.cumsum(tiles_c)
    row_start = (tile_end - tiles_c) * TM_FFN
    dest = row_start[cls] + rank
    nt = t // TM_FFN + N_CLASSES + 1
    nvalid = tile_end[-1]
    tidx = jnp.arange(nt, dtype=jnp.int32)
    tcls = jnp.sum(tile_end[None, :] <= jnp.minimum(tidx, nvalid - 1)[:, None], axis=1).astype(jnp.int32)
    ta = jnp.asarray(CLASS_EA, jnp.int32)[tcls]
    tb = jnp.asarray(CLASS_EB, jnp.int32)[tcls]
    return _moe_ffn(dest, ta, tb, nvalid.reshape(1).astype(jnp.int32), xw, w1, w3, w2, layer, g, b)


def _rope_tables(seq):
    pos = jnp.arange(seq, dtype=jnp.float32)
    inv_freq = ROPE_THETA ** (-jnp.arange(0, HEAD_DIM, 2, dtype=jnp.float32) / HEAD_DIM)
    ang = pos[:, None] * inv_freq[None, :]
    reps = LANES // (HEAD_DIM // 2)
    return jnp.tile(jnp.cos(ang), (1, reps)), jnp.tile(jnp.sin(ang), (1, reps))


def kernel(x, w_in, w_out, rpb, pool_w, pool_scale, router_w, router_bias, moe_w1, moe_w3, moe_w2, ln_g, ln_b):
    b, s, d = x.shape
    t = b * s
    bf = jnp.bfloat16
    cos_t, sin_t = _rope_tables(s)
    perm_np = _residue_perm()
    perm, perm_t = jnp.asarray(perm_np, bf), jnp.asarray(perm_np.T, bf)
    d4_rel = jnp.asarray(_d4_rel())
    rw_hi = router_w.astype(bf)
    rw_lo = (router_w - rw_hi.astype(jnp.float32)).astype(bf)
    wr_t = jnp.concatenate([rw_hi.T, rw_lo.T], axis=0)
    rbias = router_bias.astype(jnp.float32).reshape(N_EXPERTS, 1)
    w_in_bf, w_out_bf, pool_w_bf = w_in.astype(bf), w_out.astype(bf), pool_w.astype(bf)
    w1_bf, w3_bf, w2_bf = moe_w1.astype(bf), moe_w3.astype(bf), moe_w2.astype(bf)

    x2 = x.reshape(t, d)
    for layer in range(DEPTH):
        i = layer // 2
        g0, b0 = ln_g[layer, 0].reshape(1, d), ln_b[layer, 0].reshape(1, d)
        g1, b1 = ln_g[layer, 1].reshape(1, d), ln_b[layer, 1].reshape(1, d)
        if layer % 2 == 0:
            qkv, qres = _qkv_proj(x2, t, w_in_bf[i], cos_t, sin_t, perm, s)
            qkv = qkv.reshape(b, s, 3 * d)
            oa, la = _attn_d16(qres, b, s)
            oa, la = _attn_d4(qres, oa, la, d4_rel, perm_t, b, s)
            oa = _attn_d1(qkv, oa, la)
            ob = _attn_b(qkv, _bias_table(rpb[i]))
            xw, meta, cnt = _out_proj_ln(oa.reshape(t, HALF_D), ob.reshape(t, HALF_D), x2, t, w_out_bf[i], g0, b0,
                                         wr_t, rbias)
        else:
            xw, meta, cnt = _pool_ln(x2, t, pool_w_bf[i], pool_scale[i].reshape(1, d), g0, b0, wr_t, rbias, s)
        x2 = _moe_ln(xw, meta, cnt, w1_bf, w3_bf, w2_bf, layer, g1, b1)
    return x2[:t].reshape(b, s, d)
```
